```python
import math
import jax
import jax.numpy as jnp
from jax import lax
import numpy as np


D_MODEL = 1024
BATCH = 8
SEQ = 2048
DEPTH = 4
DEC_BATCH = 128
DEC_SEQ = 4
PAST_LEN = 2048
PAGE_SIZE = 128

N_MIXERS = 3
N_HEADS = 16
HEAD_DIM = 64
BRANCH = N_HEADS * HEAD_DIM
Q_BLOCK = 128
MOBA_BLOCK = 256
MOBA_TOPK = 3
MOBA_Q_CHUNK = 16
POOL_WINDOWS = (2, 4, 8, 16)
POOL_GROUPS = len(POOL_WINDOWS)
POOL_GROUP_W = BRANCH // POOL_GROUPS
POOL_HIST = max(POOL_WINDOWS) - 1
PLE_DIM = 256
RMS_EPS = 1e-6
NEG_INF = -1e30
FORGET_BIAS_LO = 1.0
FORGET_BIAS_HI = 6.0
N_FOX_LAYERS = (DEPTH + 2) // 3
N_MOBA_LAYERS = (DEPTH + 1) // 3
N_POOL_LAYERS = DEPTH // 3

kernel_name = 'fox_moba_pool_hybrid_step'

F32 = jnp.float32


def rmsnorm(x, g):
    xf = x.astype(F32)
    y = xf * lax.rsqrt(jnp.mean(xf * xf, axis=-1, keepdims=True) + RMS_EPS)
    return (y * g.astype(F32)).astype(x.dtype)


def alibi_slopes():
    return 2.0 ** (-8.0 * jnp.arange(1, N_HEADS + 1, dtype=F32) / N_HEADS)


def split_heads(t):
    return t.reshape(t.shape[:-1] + (N_HEADS, HEAD_DIM))


def rows_from_pool(cache, slot, page_table, pos):
    phys = page_table[:, pos // PAGE_SIZE]
    off = (pos % PAGE_SIZE)[None, :]
    return cache[slot, phys, off]


def ple_add(h, p, w_ple, g, w_gate):
    gate = jax.nn.sigmoid(rmsnorm(h, g) @ w_gate)
    return h + (p @ w_ple) * gate


def fox_project(hn, w_in, b_f):
    proj = hn @ w_in
    q, k, v, z = jnp.split(proj[..., :4 * BRANCH], 4, axis=-1)
    logf = jax.nn.log_sigmoid(proj[..., 4 * BRANCH:].astype(F32) + b_f.astype(F32))
    return split_heads(q), split_heads(k), split_heads(v), z, logf


def fox_attend_prompt(q, k, v, logf):
    B, S = q.shape[:2]
    scale = HEAD_DIM ** -0.5
    c = jnp.cumsum(logf, axis=1).transpose(0, 2, 1)
    nblk = S // Q_BLOCK
    qb = q.reshape(B, nblk, Q_BLOCK, N_HEADS, HEAD_DIM).transpose(1, 0, 2, 3, 4)
    cb = c.reshape(B, N_HEADS, nblk, Q_BLOCK).transpose(2, 0, 1, 3)
    kpos = jnp.arange(S)

    def block(args):
        i, qi, ci = args
        qpos = i * Q_BLOCK + jnp.arange(Q_BLOCK)
        s = jnp.einsum('bqhd,bkhd->bhqk', qi, k, preferred_element_type=F32) * scale
        s = s + ci[..., None] - c[:, :, None, :]
        s = jnp.where(kpos[None, :] <= qpos[:, None], s, NEG_INF)
        p = jax.nn.softmax(s, axis=-1).astype(v.dtype)
        return jnp.einsum('bhqk,bkhd->bqhd', p, v)

    out = lax.map(block, (jnp.arange(nblk), qb, cb))
    return out.transpose(1, 0, 2, 3, 4).reshape(B, S, BRANCH)


def fox_attend_sample(q, k, v, logf, k_past, v_past, logf_past):
    DB, T = q.shape[:2]
    P = k_past.shape[1]
    scale = HEAD_DIM ** -0.5
    c = jnp.cumsum(jnp.concatenate([logf_past.astype(F32), logf], axis=1), axis=1).transpose(0, 2, 1)
    cq = c[:, :, P:, None]
    s_past = jnp.einsum('bqhd,bkhd->bhqk', q, k_past, preferred_element_type=F32) * scale + cq - c[:, :, None, :P]
    s_new = jnp.einsum('bqhd,bkhd->bhqk', q, k, preferred_element_type=F32) * scale + cq - c[:, :, None, P:]
    tpos = jnp.arange(T)
    s_new = jnp.where(tpos[None, :] <= tpos[:, None], s_new, NEG_INF)
    p = jax.nn.softmax(jnp.concatenate([s_past, s_new], axis=-1), axis=-1)
    out = (jnp.einsum('bhqk,bkhd->bqhd', p[..., :P].astype(v.dtype), v_past.astype(v.dtype))
           + jnp.einsum('bhqk,bkhd->bqhd', p[..., P:].astype(v.dtype), v))
    return out.reshape(DB, T, BRANCH)


def moba_attend_prompt(q, k, v):
    B, S = q.shape[:2]
    L = MOBA_BLOCK
    scale = HEAD_DIM ** -0.5
    slopes = alibi_slopes()
    nb = -(-S // L)
    pad = nb * L - S
    kp = jnp.pad(k, ((0, 0), (0, pad), (0, 0), (0, 0)))
    vp = jnp.pad(v, ((0, 0), (0, pad), (0, 0), (0, 0)))
    kbh = kp.reshape(B, nb, L, N_HEADS, HEAD_DIM).transpose(0, 3, 1, 2, 4)
    vbh = vp.reshape(B, nb, L, N_HEADS, HEAD_DIM).transpose(0, 3, 1, 2, 4)
    kmean = jnp.mean(kbh.astype(F32), axis=3)
    k_sel = min(MOBA_TOPK, nb - 1)
    nchunk = S // MOBA_Q_CHUNK
    qc = q.reshape(B, nchunk, MOBA_Q_CHUNK, N_HEADS, HEAD_DIM).transpose(1, 0, 3, 2, 4)
    bi = jnp.arange(B)[:, None, None, None]
    hi = jnp.arange(N_HEADS)[None, :, None, None]
    blk_ids = jnp.arange(nb)
    lpos = jnp.arange(L)

    def chunk(args):
        i, qh = args
        qpos = i * MOBA_Q_CHUNK + jnp.arange(MOBA_Q_CHUNK)
        own = (i * MOBA_Q_CHUNK) // L
        k_own = lax.dynamic_index_in_dim(kbh, own, axis=2, keepdims=False)
        v_own = lax.dynamic_index_in_dim(vbh, own, axis=2, keepdims=False)
        opos = own * L + lpos
        dist_own = (qpos[:, None] - opos[None, :]).astype(F32)
        s_own = jnp.einsum('bhqd,bhpd->bhqp', qh, k_own, preferred_element_type=F32) * scale
        s_own = s_own - slopes[None, :, None, None] * dist_own[None, None]
        s_own = jnp.where(opos[None, :] <= qpos[:, None], s_own, NEG_INF)
        if k_sel > 0:
            gate = jnp.einsum('bhqd,bhnd->bhqn', qh.astype(F32), kmean)
            gate = jnp.where(blk_ids < own, gate, NEG_INF)
            _, idx = lax.top_k(gate, k_sel)
            ok = idx < own
            ksel = kbh[bi, hi, idx]
            vsel = vbh[bi, hi, idx]
            spos = idx[..., None] * L + lpos
            dist = (qpos[None, None, :, None, None] - spos).astype(F32)
            s_sel = jnp.einsum('bhqd,bhqjpd->bhqjp', qh, ksel, preferred_element_type=F32) * scale
            s_sel = s_sel - slopes[None, :, None, None, None] * dist
            s_sel = jnp.where(ok[..., None], s_sel, NEG_INF).reshape(B, N_HEADS, MOBA_Q_CHUNK, k_sel * L)
            p = jax.nn.softmax(jnp.concatenate([s_sel, s_own], axis=-1), axis=-1)
            p_sel = p[..., :k_sel * L].reshape(B, N_HEADS, MOBA_Q_CHUNK, k_sel, L).astype(v.dtype)
            p_own = p[..., k_sel * L:].astype(v.dtype)
            out = (jnp.einsum('bhqjp,bhqjpd->bhqd', p_sel, vsel)
                   + jnp.einsum('bhqp,bhpd->bhqd', p_own, v_own))
        else:
            p_own = jax.nn.softmax(s_own, axis=-1).astype(v.dtype)
            out = jnp.einsum('bhqp,bhpd->bhqd', p_own, v_own)
        return out.transpose(0, 2, 1, 3)

    out = lax.map(chunk, (jnp.arange(nchunk), qc))
    return out.transpose(1, 0, 2, 3, 4).reshape(B, S, BRANCH)


def moba_attend_sample(q, k, v, cache_k, cache_v, slot, page_table):
    DB, T = q.shape[:2]
    L = MOBA_BLOCK
    scale = HEAD_DIM ** -0.5
    slopes = alibi_slopes()
    nfp = PAST_LEN // L
    own_start = nfp * L
    n_sel = min(MOBA_TOPK, nfp)
    qpos = PAST_LEN + jnp.arange(T)
    qh = q.transpose(0, 2, 1, 3)
    lpos = jnp.arange(L)
    R = PAST_LEN - own_start
    opos_past = own_start + jnp.arange(R)
    k_own = jnp.concatenate([rows_from_pool(cache_k, slot, page_table, opos_past).astype(k.dtype), k], axis=1)
    v_own = jnp.concatenate([rows_from_pool(cache_v, slot, page_table, opos_past).astype(v.dtype), v], axis=1)
    opos = own_start + jnp.arange(R + T)
    dist_own = (qpos[:, None] - opos[None, :]).astype(F32)
    s_own = jnp.einsum('bhqd,bphd->bhqp', qh, k_own, preferred_element_type=F32) * scale
    s_own = s_own - slopes[None, :, None, None] * dist_own[None, None]
    s_own = jnp.where(opos[None, :] <= qpos[:, None], s_own, NEG_INF)
    if n_sel > 0:
        k_past = cache_k[slot, page_table].reshape(DB, PAST_LEN, N_HEADS, HEAD_DIM)
        kmean = jnp.mean(k_past[:, :own_start].astype(F32).reshape(DB, nfp, L, N_HEADS, HEAD_DIM), axis=2)
        gate = jnp.einsum('bhqd,bnhd->bhqn', qh.astype(F32), kmean)
        _, idx = lax.top_k(gate, n_sel)
        spos = idx[..., None] * L + lpos
        bi = jnp.arange(DB)[:, None, None, None, None]
        hi = jnp.arange(N_HEADS)[None, :, None, None, None]
        phys = page_table[bi, spos // PAGE_SIZE]
        off = spos % PAGE_SIZE
        ksel = cache_k[slot, phys, off, hi].astype(k.dtype)
        vsel = cache_v[slot, phys, off, hi].astype(v.dtype)
        dist = (qpos[None, None, :, None, None] - spos).astype(F32)
        s_sel = jnp.einsum('bhqd,bhqjpd->bhqjp', qh, ksel, preferred_element_type=F32) * scale
        s_sel = (s_sel - slopes[None, :, None, None, None] * dist).reshape(DB, N_HEADS, T, n_sel * L)
        p = jax.nn.softmax(jnp.concatenate([s_sel, s_own], axis=-1), axis=-1)
        p_sel = p[..., :n_sel * L].reshape(DB, N_HEADS, T, n_sel, L).astype(v.dtype)
        p_own = p[..., n_sel * L:].astype(v.dtype)
        out = (jnp.einsum('bhqjp,bhqjpd->bhqd', p_sel, vsel)
               + jnp.einsum('bhqp,bphd->bhqd', p_own, v_own))
    else:
        p_own = jax.nn.softmax(s_own, axis=-1).astype(v.dtype)
        out = jnp.einsum('bhqp,bphd->bhqd', p_own, v_own)
    return out.transpose(0, 2, 1, 3).reshape(DB, T, BRANCH)


def pool_mix(u, hist, start_pos, w_group, scale):
    B, T = u.shape[:2]
    ext = jnp.concatenate([hist.astype(u.dtype), u], axis=1)
    extf = ext.astype(F32)
    cs = jnp.concatenate([jnp.zeros((B, 1, BRANCH), F32), jnp.cumsum(extf, axis=1)], axis=1)
    pos = start_pos + jnp.arange(T)
    diffs = []
    for gi, w in enumerate(POOL_WINDOWS):
        lo, hi = gi * POOL_GROUP_W, (gi + 1) * POOL_GROUP_W
        win_sum = cs[:, POOL_HIST + 1:POOL_HIST + 1 + T, lo:hi] - cs[:, POOL_HIST + 1 - w:POOL_HIST + 1 - w + T, lo:hi]
        cnt = jnp.minimum(pos + 1, w).astype(F32)[None, :, None]
        diffs.append(win_sum / cnt - extf[:, POOL_HIST:, lo:hi])
    d = jnp.stack(diffs, axis=2).astype(u.dtype)
    y = jnp.einsum('btgc,gcd->btgd', d, w_group).reshape(B, T, BRANCH) * scale
    return y, ext[:, -POOL_HIST:]


def setup_inputs(seed: int = 0) -> dict:
    key = jax.random.key(seed)
    ks = jax.random.split(key, 26)
    n_pages = PAST_LEN // PAGE_SIZE
    used = DEC_BATCH * n_pages
    n_phys = used + max(1, used // 4)

    def nrm(k, shape, s=1.0):
        return jax.random.normal(k, shape, F32) * s

    def gain(k, shape):
        return 1.0 + 0.02 * jax.random.normal(k, shape, F32)

    kv_f = (N_FOX_LAYERS, n_phys, PAGE_SIZE, N_HEADS, HEAD_DIM)
    kv_m = (N_MOBA_LAYERS, n_phys, PAGE_SIZE, N_HEADS, HEAD_DIM)
    lf_shape = (N_FOX_LAYERS, n_phys, PAGE_SIZE, N_HEADS)
    page_table = jax.random.permutation(ks[0], n_phys)[:used].reshape(DEC_BATCH, n_pages).astype(jnp.int32)
    return {
        'x_prompt': nrm(ks[1], (BATCH, SEQ, D_MODEL)),
        'x_sample': nrm(ks[2], (DEC_BATCH, DEC_SEQ, D_MODEL)),
        'cache_k_fox': nrm(ks[3], kv_f),
        'cache_v_fox': nrm(ks[4], kv_f),
        'cache_logf_fox': jax.nn.log_sigmoid(jax.random.uniform(ks[5], lf_shape, F32, FORGET_BIAS_LO, FORGET_BIAS_HI) + nrm(ks[6], lf_shape)),
        'cache_k_moba': nrm(ks[7], kv_m),
        'cache_v_moba': nrm(ks[8], kv_m),
        'state_pool': nrm(ks[9], (N_POOL_LAYERS, DEC_BATCH, POOL_HIST, BRANCH)),
        'page_table': page_table,
        'p_prompt': nrm(ks[10], (DEPTH, BATCH, SEQ, PLE_DIM)),
        'p_sample': nrm(ks[11], (DEPTH, DEC_BATCH, DEC_SEQ, PLE_DIM)),
        'norm_mix': gain(ks[12], (DEPTH, D_MODEL)),
        'w_in_fox': nrm(ks[13], (N_FOX_LAYERS, D_MODEL, 4 * BRANCH + N_HEADS), D_MODEL ** -0.5),
        'b_forget': jax.random.uniform(ks[14], (N_FOX_LAYERS, N_HEADS), F32, FORGET_BIAS_LO, FORGET_BIAS_HI),
        'w_in_moba': nrm(ks[15], (N_MOBA_LAYERS, D_MODEL, 4 * BRANCH), D_MODEL ** -0.5),
        'w_in_pool': nrm(ks[16], (N_POOL_LAYERS, D_MODEL, 2 * BRANCH), D_MODEL ** -0.5),
        'w_pool_group': nrm(ks[17], (N_POOL_LAYERS, POOL_GROUPS, POOL_GROUP_W, POOL_GROUP_W), POOL_GROUP_W ** -0.5),
        'pool_scale': 1.0 + 0.1 * jax.random.normal(ks[18], (N_POOL_LAYERS, BRANCH), F32),
        'w_out': nrm(ks[19], (DEPTH, BRANCH, D_MODEL), BRANCH ** -0.5),
        'w_ple': nrm(ks[20], (DEPTH, PLE_DIM, D_MODEL), PLE_DIM ** -0.5),
        'norm_ple': gain(ks[21], (DEPTH, D_MODEL)),
        'w_ple_gate': nrm(ks[22], (DEPTH, D_MODEL, D_MODEL), D_MODEL ** -0.5),
        'norm_final': gain(ks[23], (D_MODEL,)),
    }


def reference(x_prompt, x_sample, cache_k_fox, cache_v_fox, cache_logf_fox, cache_k_moba, cache_v_moba,
              state_pool, page_table, p_prompt, p_sample, norm_mix, w_in_fox, b_forget, w_in_moba,
              w_in_pool, w_pool_group, pool_scale, w_out, w_ple, norm_ple, w_ple_gate, norm_final):
    dec_b = x_sample.shape[0]
    hp, hs = x_prompt, x_sample
    kf_p, vf_p, lf_p, km_p, vm_p, pl_p = [], [], [], [], [], []
    kf_s, vf_s, lf_s, km_s, vm_s, pl_s = [], [], [], [], [], []
    for i in range(DEPTH):
        kind, slot = i % N_MIXERS, i // N_MIXERS
        hn_p = rmsnorm(hp, norm_mix[i])
        hn_s = rmsnorm(hs, norm_mix[i])
        if kind == 0:
            q, k, v, z, lf = fox_project(hn_p, w_in_fox[slot], b_forget[slot])
            o_p = fox_attend_prompt(q, k, v, lf) * jax.nn.silu(z)
            kf_p.append(k); vf_p.append(v); lf_p.append(lf)
            q, k, v, z, lf = fox_project(hn_s, w_in_fox[slot], b_forget[slot])
            k_past = cache_k_fox[slot, page_table].reshape(dec_b, PAST_LEN, N_HEADS, HEAD_DIM)
            v_past = cache_v_fox[slot, page_table].reshape(dec_b, PAST_LEN, N_HEADS, HEAD_DIM)
            lf_past = cache_logf_fox[slot, page_table].reshape(dec_b, PAST_LEN, N_HEADS)
            o_s = fox_attend_sample(q, k.astype(k_past.dtype), v, lf, k_past, v_past, lf_past) * jax.nn.silu(z)
            kf_s.append(k); vf_s.append(v); lf_s.append(lf)
        elif kind == 1:
            q, k, v, z = jnp.split(hn_p @ w_in_moba[slot], 4, axis=-1)
            q, k, v = split_heads(q), split_heads(k), split_heads(v)
            o_p = moba_attend_prompt(q, k, v) * jax.nn.silu(z)
            km_p.append(k); vm_p.append(v)
            q, k, v, z = jnp.split(hn_s @ w_in_moba[slot], 4, axis=-1)
            q, k, v = split_heads(q), split_heads(k), split_heads(v)
            o_s = moba_attend_sample(q, k, v, cache_k_moba, cache_v_moba, slot, page_table) * jax.nn.silu(z)
            km_s.append(k); vm_s.append(v)
        else:
            u, z = jnp.split(hn_p @ w_in_pool[slot], 2, axis=-1)
            zero_hist = jnp.zeros((u.shape[0], POOL_HIST, BRANCH), u.dtype)
            y, hist_p = pool_mix(u, zero_hist, 0, w_pool_group[slot], pool_scale[slot])
            o_p = y * jax.nn.silu(z)
            pl_p.append(hist_p)
            u, z = jnp.split(hn_s @ w_in_pool[slot], 2, axis=-1)
            y, hist_s = pool_mix(u, state_pool[slot], PAST_LEN, w_pool_group[slot], pool_scale[slot])
            o_s = y * jax.nn.silu(z)
            pl_s.append(hist_s)
        hp = hp + o_p @ w_out[i]
        hs = hs + o_s @ w_out[i]
        hp = ple_add(hp, p_prompt[i], w_ple[i], norm_ple[i], w_ple_gate[i])
        hs = ple_add(hs, p_sample[i], w_ple[i], norm_ple[i], w_ple_gate[i])
    y_prompt = rmsnorm(hp, norm_final)
    y_sample = rmsnorm(hs, norm_final)
    return (y_prompt, y_sample,
            jnp.stack(kf_p), jnp.stack(vf_p), jnp.stack(lf_p), jnp.stack(km_p), jnp.stack(vm_p), jnp.stack(pl_p),
            jnp.stack(kf_s), jnp.stack(vf_s), jnp.stack(lf_s), jnp.stack(km_s), jnp.stack(vm_s), jnp.stack(pl_s))
```

```python
import functools
import math

import jax
import jax.numpy as jnp
from jax import lax
from jax.experimental import pallas as pl
from jax.experimental.pallas import tpu as pltpu

F32 = jnp.float32
BF16 = jnp.bfloat16

D_MODEL = 1024
N_HEADS = 16
HEAD_DIM = 64
BRANCH = N_HEADS * HEAD_DIM
PAGE_SIZE = 128
MOBA_BLOCK = 256
MOBA_TOPK = 3
POOL_WINDOWS = (2, 4, 8, 16)
POOL_GROUP_W = BRANCH // len(POOL_WINDOWS)
POOL_HIST = max(POOL_WINDOWS) - 1
N_MIXERS = 3
RMS_EPS = 1e-6
NEG_INF = -1e30
ATTN_SCALE = HEAD_DIM ** -0.5

V7X_LANES = 128
V7X_SUBLANES = 8
V7X_VMEM_BYTES = 64 * 1024 * 1024
VMEM_CEILING = V7X_VMEM_BYTES - 8 * 1024 * 1024

HEADS_PER_VREG = V7X_LANES // HEAD_DIM
N_HEAD_PAIRS = N_HEADS // HEADS_PER_VREG
PAGE_FLAT = PAGE_SIZE * N_HEADS
PAGES_PER_STEP = 4
ROW_TILE = 512
ATTN_TILE = 256


def _vmem_limit(block_bytes, scratch_bytes=0):
    est = 2 * block_bytes + scratch_bytes
    return int(min(VMEM_CEILING, max(32 * 1024 * 1024, est * 3 // 2)))


def _nt_dot(a, b):
    return lax.dot_general(a, b, (((1,), (1,)), ((), ())), preferred_element_type=F32)


def _rmsnorm(x, g):
    return x * lax.rsqrt(jnp.mean(x * x, axis=-1, keepdims=True) + RMS_EPS) * g


def _norm_proj_body(*refs, has_forget):
    if has_forget:
        x_ref, g_ref, w_ref, wf_ref, bf_ref, o_ref, lf_ref, xn_ref = refs
    else:
        x_ref, g_ref, w_ref, o_ref, xn_ref = refs

    @pl.when(pl.program_id(1) == 0)
    def _():
        xn = _rmsnorm(x_ref[...], g_ref[...]).astype(BF16)
        xn_ref[...] = xn
        if has_forget:
            raw = jnp.dot(xn, wf_ref[...], preferred_element_type=F32) + bf_ref[...]
            lf_ref[...] = jax.nn.log_sigmoid(raw)

    o_ref[...] = jnp.dot(xn_ref[...], w_ref[...], preferred_element_type=F32)


def _norm_proj(x, g, w, wf=None, bf=None):
    n, d = x.shape
    n_out = w.shape[1]
    tm = ROW_TILE
    tn = min(n_out, 2048)
    has_forget = wf is not None
    in_specs = [
        pl.BlockSpec((tm, d), lambda i, j: (i, 0)),
        pl.BlockSpec((1, d), lambda i, j: (0, 0)),
        pl.BlockSpec((d, tn), lambda i, j: (0, j)),
    ]
    out_shape = [jax.ShapeDtypeStruct((n, n_out), F32)]
    out_specs = [pl.BlockSpec((tm, tn), lambda i, j: (i, j))]
    args = [x, g, w]
    block_bytes = tm * d * 4 + d * tn * 2 + tm * tn * 4
    if has_forget:
        in_specs += [pl.BlockSpec((d, V7X_LANES), lambda i, j: (0, 0)),
                     pl.BlockSpec((1, V7X_LANES), lambda i, j: (0, 0))]
        out_shape.append(jax.ShapeDtypeStruct((n, V7X_LANES), F32))
        out_specs.append(pl.BlockSpec((tm, V7X_LANES), lambda i, j: (i, 0)))
        args += [wf, bf]
        block_bytes += d * V7X_LANES * 2 + tm * V7X_LANES * 4
    res = pl.pallas_call(
        functools.partial(_norm_proj_body, has_forget=has_forget),
        grid=(n // tm, n_out // tn),
        in_specs=in_specs,
        out_specs=out_specs,
        out_shape=out_shape,
        scratch_shapes=[pltpu.VMEM((tm, d), BF16)],
        compiler_params=pltpu.CompilerParams(
            dimension_semantics=("arbitrary", "arbitrary"),
            vmem_limit_bytes=_vmem_limit(block_bytes, tm * d * 2)),
        name="norm_proj_forget" if has_forget else "norm_proj",
    )(*args)
    return res if has_forget else res[0]


def _post_body(*refs, final_norm):
    if final_norm:
        (a_ref, z_ref, h_ref, p_ref, wo_ref, wp_ref, wg_ref, g_ref, gf_ref, o_ref) = refs
    else:
        (a_ref, z_ref, h_ref, p_ref, wo_ref, wp_ref, wg_ref, g_ref, o_ref) = refs
    gated = (a_ref[...] * jax.nn.silu(z_ref[...])).astype(BF16)
    h1 = h_ref[...] + jnp.dot(gated, wo_ref[...], preferred_element_type=F32)
    hn = _rmsnorm(h1, g_ref[...]).astype(BF16)
    gate = jax.nn.sigmoid(jnp.dot(hn, wg_ref[...], preferred_element_type=F32))
    pp = jnp.dot(p_ref[...].astype(BF16), wp_ref[...], preferred_element_type=F32)
    h2 = h1 + pp * gate
    if final_norm:
        h2 = _rmsnorm(h2, gf_ref[...])
    o_ref[...] = h2


def _post(a, proj, z_col, h, p, w_out, w_ple, w_gate, g_ple, g_final=None):
    n, d = h.shape
    tm = ROW_TILE
    ple = p.shape[1]
    final_norm = g_final is not None
    row = lambda i: (i, 0)
    const = lambda i: (0, 0)
    in_specs = [
        pl.BlockSpec((tm, BRANCH), row),
        pl.BlockSpec((tm, BRANCH), lambda i: (i, z_col)),
        pl.BlockSpec((tm, d), row),
        pl.BlockSpec((tm, ple), row),
        pl.BlockSpec((BRANCH, d), const),
        pl.BlockSpec((ple, d), const),
        pl.BlockSpec((d, d), const),
        pl.BlockSpec((1, d), const),
    ]
    args = [a, proj, h, p, w_out, w_ple, w_gate, g_ple]
    if final_norm:
        in_specs.append(pl.BlockSpec((1, d), const))
        args.append(g_final)
    block_bytes = (3 * tm * d + tm * ple + tm * d) * 4 + (BRANCH * d + ple * d + d * d) * 2
    return pl.pallas_call(
        functools.partial(_post_body, final_norm=final_norm),
        grid=(n // tm,),
        in_specs=in_specs,
        out_specs=pl.BlockSpec((tm, d), row),
        out_shape=jax.ShapeDtypeStruct((n, d), F32),
        compiler_params=pltpu.CompilerParams(
            dimension_semantics=("arbitrary",),
            vmem_limit_bytes=_vmem_limit(block_bytes)),
        name="post_final" if final_norm else "post",
    )(*args)


def _fox_decay_body(lf_ref, ct_ref):
    seq = lf_ref.shape[0]
    n_tiles, _, tk = ct_ref.shape
    x = lf_ref[...].T[:N_HEADS, :]
    lane = lax.broadcasted_iota(jnp.int32, x.shape, 1)
    shift = 1
    while shift < seq:
        x = x + jnp.where(lane >= shift, pltpu.roll(x, shift, 1), 0.0)
        shift *= 2
    for c in range(n_tiles):
        ct_ref[c] = x[:, c * tk:(c + 1) * tk]


def _fox_decay(lf):
    b, s, _ = lf.shape
    n_tiles = s // ATTN_TILE
    return pl.pallas_call(
        _fox_decay_body,
        grid=(b,),
        in_specs=[pl.BlockSpec((None, s, V7X_LANES), lambda i: (i, 0, 0))],
        out_specs=pl.BlockSpec((None, n_tiles, N_HEADS, ATTN_TILE), lambda i: (i, 0, 0, 0)),
        out_shape=jax.ShapeDtypeStruct((b, n_tiles, N_HEADS, ATTN_TILE), F32),
        compiler_params=pltpu.CompilerParams(
            dimension_semantics=("arbitrary",),
            vmem_limit_bytes=_vmem_limit(s * V7X_LANES * 4 * 2)),
        name="fox_decay",
    )(lf)


def _fox_prompt_body(q_ref, k_ref, v_ref, ct_ref, o_ref, kb_ref, vb_ref, acc_ref, m_ref, l_ref):
    hp = pl.program_id(1)
    qi = pl.program_id(2)
    tq = q_ref.shape[0]
    tk = tq

    @pl.when(qi == 0)
    def _():
        kb_ref[...] = k_ref[...].astype(BF16)
        vb_ref[...] = v_ref[...].astype(BF16)

    q = q_ref[...] * ATTN_SCALE
    lane = lax.broadcasted_iota(jnp.int32, q.shape, 1)
    in_head = (lane < HEAD_DIM, lane >= HEAD_DIM)
    qm = [jnp.where(in_head[i], q, 0.0).astype(BF16) for i in range(HEADS_PER_VREG)]

    m_ref[...] = jnp.full(m_ref.shape, NEG_INF, F32)
    l_ref[...] = jnp.zeros(l_ref.shape, F32)
    acc_ref[...] = jnp.zeros(acc_ref.shape, F32)

    def step(kj, diagonal):
        k0 = pl.multiple_of(kj * tk, tk)
        kblk = kb_ref[pl.ds(k0, tk), :]
        vblk = vb_ref[pl.ds(k0, tk), :]
        for i in range(HEADS_PER_VREG):
            s = _nt_dot(qm[i], kblk)
            s = s - ct_ref[kj, pl.ds(HEADS_PER_VREG * hp + i, 1), :]
            if diagonal:
                row = lax.broadcasted_iota(jnp.int32, s.shape, 0)
                col = lax.broadcasted_iota(jnp.int32, s.shape, 1)
                s = jnp.where(col <= row, s, NEG_INF)
            m_old = m_ref[i]
            m_new = jnp.maximum(m_old, jnp.max(s, axis=1, keepdims=True))
            alpha = jnp.exp(m_old - m_new)
            p = jnp.exp(s - m_new)
            l_ref[i] = alpha * l_ref[i] + jnp.sum(p, axis=1, keepdims=True)
            acc_ref[i] = alpha * acc_ref[i] + jnp.dot(p.astype(BF16), vblk, preferred_element_type=F32)
            m_ref[i] = m_new

    def past(kj, carry):
        step(kj, False)
        return carry

    lax.fori_loop(0, qi, past, 0)
    step(qi, True)
    o_ref[...] = jnp.where(in_head[0], acc_ref[0] / l_ref[0], acc_ref[1] / l_ref[1])


def _fox_prompt(proj, ct):
    b, s, _ = proj.shape
    tq = ATTN_TILE
    cols = BRANCH // V7X_LANES
    block_bytes = (tq * V7X_LANES * 2 + 2 * s * V7X_LANES + N_HEADS * s) * 4
    scratch_bytes = 2 * s * V7X_LANES * 2 + 6 * tq * V7X_LANES * 4
    return pl.pallas_call(
        _fox_prompt_body,
        grid=(b, N_HEAD_PAIRS, s // tq),
        in_specs=[
            pl.BlockSpec((None, tq, V7X_LANES), lambda bi, hp, qi: (bi, qi, hp)),
            pl.BlockSpec((None, s, V7X_LANES), lambda bi, hp, qi: (bi, 0, cols + hp)),
            pl.BlockSpec((None, s, V7X_LANES), lambda bi, hp, qi: (bi, 0, 2 * cols + hp)),
            pl.BlockSpec((None, s // tq, N_HEADS, tq), lambda bi, hp, qi: (bi, 0, 0, 0)),
        ],
        out_specs=pl.BlockSpec((None, tq, V7X_LANES), lambda bi, hp, qi: (bi, qi, hp)),
        out_shape=jax.ShapeDtypeStruct((b, s, BRANCH), F32),
        scratch_shapes=[
            pltpu.VMEM((s, V7X_LANES), BF16),
            pltpu.VMEM((s, V7X_LANES), BF16),
            pltpu.VMEM((HEADS_PER_VREG, tq, V7X_LANES), F32),
            pltpu.VMEM((HEADS_PER_VREG, tq, 1), F32),
            pltpu.VMEM((HEADS_PER_VREG, tq, 1), F32),
        ],
        compiler_params=pltpu.CompilerParams(
            dimension_semantics=("arbitrary", "arbitrary", "arbitrary"),
            vmem_limit_bytes=_vmem_limit(block_bytes, scratch_bytes)),
        name="fox_prompt",
    )(proj, proj, proj, ct)


def _split_bf16(x):
    hi = x.astype(BF16)
    lo = (x - hi.astype(F32)).astype(BF16)
    return hi, lo


def _moba_prompt_body(slope_ref, q_ref, k_ref, v_ref, o_ref,
                      kb_ref, vt_ref, kmean_ref, acc_ref, m_ref, l_ref, sel_ref):
    hp = pl.program_id(1)
    qi = pl.program_id(2)
    blk = MOBA_BLOCK
    nb = kb_ref.shape[0]

    @pl.when(qi == 0)
    def _():
        kmean_ref[...] = jnp.zeros(kmean_ref.shape, F32)
        for j in range(nb):
            kf = k_ref[pl.ds(j * blk, blk), :]
            kb_ref[j] = kf.astype(BF16)
            vt_ref[j] = v_ref[pl.ds(j * blk, blk), :].T.astype(BF16)
            kmean_ref[pl.ds(j, 1), :] = jnp.mean(kf, axis=0, keepdims=True)

    q = q_ref[...]
    lane = lax.broadcasted_iota(jnp.int32, q.shape, 1)
    in_head = (lane < HEAD_DIM, lane >= HEAD_DIM)
    qh = [jnp.where(in_head[i], q, 0.0) for i in range(HEADS_PER_VREG)]
    qm = [(qh[i] * ATTN_SCALE).astype(BF16) for i in range(HEADS_PER_VREG)]

    km_hi, km_lo = _split_bf16(kmean_ref[...])
    jrow = lax.broadcasted_iota(jnp.int32, (kmean_ref.shape[0], blk), 0)
    valid = jrow < qi
    for i in range(HEADS_PER_VREG):
        q_hi, q_lo = _split_bf16(qh[i])
        gate = _nt_dot(km_hi, q_hi) + (_nt_dot(km_hi, q_lo) + _nt_dot(km_lo, q_hi))
        gate = jnp.where(valid, gate, NEG_INF)
        rank = jnp.zeros(gate.shape, jnp.int32)
        for jp in range(nb):
            gj = gate[jp:jp + 1, :]
            beats = (gj > gate) | ((gj == gate) & (jp < jrow))
            rank = rank + beats.astype(jnp.int32)
        sel_ref[i] = jnp.where(valid & (rank < MOBA_TOPK), 1.0, 0.0)

    m_ref[...] = jnp.full(m_ref.shape, NEG_INF, F32)
    l_ref[...] = jnp.zeros(l_ref.shape, F32)
    acc_ref[...] = jnp.zeros(acc_ref.shape, F32)

    kloc = lax.broadcasted_iota(jnp.int32, (blk, blk), 0)
    qloc = lax.broadcasted_iota(jnp.int32, (blk, blk), 1)
    rel = (qloc - kloc).astype(F32)

    def step(j, own):
        kblk = kb_ref[j]
        vt = vt_ref[j]
        for i in range(HEADS_PER_VREG):
            slope = slope_ref[HEADS_PER_VREG * hp + i]
            st = _nt_dot(kblk, qm[i])
            dist = rel + ((qi - j) * blk).astype(F32)
            st = st - slope * dist
            if own:
                st = jnp.where(kloc <= qloc, st, NEG_INF)
            else:
                st = jnp.where(sel_ref[i, pl.ds(j, 1), :] > 0.5, st, NEG_INF)
            m_old = m_ref[i]
            m_new = jnp.maximum(m_old, jnp.max(st, axis=0, keepdims=True))
            alpha = jnp.exp(m_old - m_new)
            p = jnp.exp(st - m_new)
            l_ref[i] = alpha * l_ref[i] + jnp.sum(p, axis=0, keepdims=True)
            acc_ref[i] = alpha * acc_ref[i] + jnp.dot(vt, p.astype(BF16), preferred_element_type=F32)
            m_ref[i] = m_new

    step(qi, True)

    def past(j, carry):
        step(j, False)
        return carry

    lax.fori_loop(0, qi, past, 0)
    row = lax.broadcasted_iota(jnp.int32, acc_ref.shape[1:], 0)
    out_t = jnp.where(row < HEAD_DIM, acc_ref[0] / l_ref[0], acc_ref[1] / l_ref[1])
    o_ref[...] = out_t.T


def _moba_prompt(proj, slopes):
    b, s, _ = proj.shape
    blk = MOBA_BLOCK
    nb = s // blk
    cols = BRANCH // V7X_LANES
    block_bytes = (blk * V7X_LANES * 2 + 2 * s * V7X_LANES) * 4
    scratch_bytes = 2 * s * V7X_LANES * 2 + 2 * V7X_LANES * blk * 4 * 2
    return pl.pallas_call(
        _moba_prompt_body,
        grid=(b, N_HEAD_PAIRS, nb),
        in_specs=[
            pl.BlockSpec(memory_space=pltpu.SMEM),
            pl.BlockSpec((None, blk, V7X_LANES), lambda bi, hp, qi: (bi, qi, hp)),
            pl.BlockSpec((None, s, V7X_LANES), lambda bi, hp, qi: (bi, 0, cols + hp)),
            pl.BlockSpec((None, s, V7X_LANES), lambda bi, hp, qi: (bi, 0, 2 * cols + hp)),
        ],
        out_specs=pl.BlockSpec((None, blk, V7X_LANES), lambda bi, hp, qi: (bi, qi, hp)),
        out_shape=jax.ShapeDtypeStruct((b, s, BRANCH), F32),
        scratch_shapes=[
            pltpu.VMEM((nb, blk, V7X_LANES), BF16),
            pltpu.VMEM((nb, V7X_LANES, blk), BF16),
            pltpu.VMEM((2 * V7X_SUBLANES, V7X_LANES), F32),
            pltpu.VMEM((HEADS_PER_VREG, V7X_LANES, blk), F32),
            pltpu.VMEM((HEADS_PER_VREG, 1, blk), F32),
            pltpu.VMEM((HEADS_PER_VREG, 1, blk), F32),
            pltpu.VMEM((HEADS_PER_VREG, 2 * V7X_SUBLANES, blk), F32),
        ],
        compiler_params=pltpu.CompilerParams(
            dimension_semantics=("arbitrary", "arbitrary", "arbitrary"),
            vmem_limit_bytes=_vmem_limit(block_bytes, scratch_bytes)),
        name="moba_prompt",
    )(slopes, proj, proj, proj)


def _pool_body(*refs, start_pos, has_hist):
    if has_hist:
        u_ref, hist_ref, wg_ref, sc_ref, y_ref, ho_ref, ext_ref = refs
    else:
        u_ref, wg_ref, sc_ref, y_ref, ho_ref, ext_ref = refs
    t = pl.program_id(1)
    ts = u_ref.shape[0]
    pad = POOL_HIST + 1

    @pl.when(t == 0)
    def _():
        if has_hist:
            ext_ref[0:1, :] = jnp.zeros((1, BRANCH), F32)
            ext_ref[1:pad, :] = hist_ref[...]
        else:
            ext_ref[0:pad, :] = jnp.zeros((pad, BRANCH), F32)

    @pl.when(t > 0)
    def _():
        ext_ref[0:pad, :] = ext_ref[ts:ts + pad, :]

    u = u_ref[...]
    ext_ref[pad:pad + ts, :] = u
    pos = start_pos + t * ts + lax.broadcasted_iota(jnp.int32, (ts, 1), 0)
    for gi, w in enumerate(POOL_WINDOWS):
        c0 = gi * POOL_GROUP_W
        win = ext_ref[pl.ds(pad, ts), pl.ds(c0, POOL_GROUP_W)]
        for back in range(1, w):
            win = win + ext_ref[pl.ds(pad - back, ts), pl.ds(c0, POOL_GROUP_W)]
        cnt = jnp.minimum(pos + 1, w).astype(F32)
        d = win / cnt - u[:, c0:c0 + POOL_GROUP_W]
        y = jnp.dot(d.astype(BF16), wg_ref[gi], preferred_element_type=F32)
        y_ref[:, c0:c0 + POOL_GROUP_W] = y * sc_ref[:, c0:c0 + POOL_GROUP_W]

    @pl.when(t == pl.num_programs(1) - 1)
    def _():
        ho_ref[...] = ext_ref[ts + 1:ts + pad, :]


def _pool(proj, hist, w_group, scale, start_pos):
    b, t, _ = proj.shape
    ts = min(t, ROW_TILE)
    has_hist = hist is not None
    pad = POOL_HIST + 1
    in_specs = [pl.BlockSpec((None, ts, BRANCH), lambda bi, ti: (bi, ti, 0))]
    args = [proj]
    if has_hist:
        in_specs.append(pl.BlockSpec((None, POOL_HIST, BRANCH), lambda bi, ti: (bi, 0, 0)))
        args.append(hist)
    in_specs += [
        pl.BlockSpec((len(POOL_WINDOWS), POOL_GROUP_W, POOL_GROUP_W), lambda bi, ti: (0, 0, 0)),
        pl.BlockSpec((1, BRANCH), lambda bi, ti: (0, 0)),
    ]
    args += [w_group, scale]
    block_bytes = (2 * ts * BRANCH + 2 * pad * BRANCH) * 4 + w_group.size * 2
    return pl.pallas_call(
        functools.partial(_pool_body, start_pos=start_pos, has_hist=has_hist),
        grid=(b, t // ts),
        in_specs=in_specs,
        out_specs=[
            pl.BlockSpec((None, ts, BRANCH), lambda bi, ti: (bi, ti, 0)),
            pl.BlockSpec((None, POOL_HIST, BRANCH), lambda bi, ti: (bi, 0, 0)),
        ],
        out_shape=[
            jax.ShapeDtypeStruct((b, t, BRANCH), F32),
            jax.ShapeDtypeStruct((b, POOL_HIST, BRANCH), F32),
        ],
        scratch_shapes=[pltpu.VMEM((ts + pad, BRANCH), F32)],
        compiler_params=pltpu.CompilerParams(
            dimension_semantics=("arbitrary", "arbitrary"),
            vmem_limit_bytes=_vmem_limit(block_bytes, (ts + pad) * BRANCH * 4)),
        name="pool_hist" if has_hist else "pool",
    )(*args)


def _lanes_same_head(x, op):
    for shift in (N_HEADS, 2 * N_HEADS, 4 * N_HEADS):
        x = op(x, pltpu.roll(x, shift, 1))
    return x


def _chunks(x):
    return [x[:, c * V7X_LANES:(c + 1) * V7X_LANES] for c in range(x.shape[1] // V7X_LANES)]


def _tile_lanes(x, width):
    return jnp.concatenate([x] * (width // x.shape[1]), axis=1)


def _head_match_mask(rows, width):
    r = lax.broadcasted_iota(jnp.int32, (rows, width), 0)
    c = lax.broadcasted_iota(jnp.int32, (rows, width), 1)
    return (r % N_HEADS) == (c % N_HEADS)


def _diag_scores(q_all, k2, n_q):
    rt = _nt_dot(q_all, k2)
    rt = jnp.where(_head_match_mask(rt.shape[0], rt.shape[1]), rt, 0.0)
    out_shape = (V7X_SUBLANES, rt.shape[1])
    row = lax.broadcasted_iota(jnp.int32, out_shape, 0)
    out = jnp.zeros(out_shape, F32)
    for t in range(n_q):
        s_t = jnp.sum(rt[t * N_HEADS:(t + 1) * N_HEADS, :], axis=0, keepdims=True)
        out = jnp.where(row == t, jnp.broadcast_to(s_t, out_shape), out)
    return out


def _expand_probs(p, n_q):
    mask = _head_match_mask(N_HEADS, p.shape[1])
    parts = [jnp.where(mask, jnp.broadcast_to(p[t:t + 1, :], mask.shape), 0.0) for t in range(n_q)]
    return jnp.concatenate(parts, axis=0).astype(BF16)


def _pad_rows(x, rows):
    return jnp.concatenate([x, jnp.zeros((rows - x.shape[0], x.shape[1]), x.dtype)], axis=0)


def _sample_softmax(s_ref, sn, n_pages):
    m = sn
    for pg in range(n_pages):
        m = functools.reduce(jnp.maximum, _chunks(s_ref[pg]), m)
    m = _lanes_same_head(m, jnp.maximum)
    m_page = _tile_lanes(m, PAGE_FLAT)
    en = jnp.exp(sn - m)
    tot = en
    for pg in range(n_pages):
        e = jnp.exp(s_ref[pg] - m_page)
        s_ref[pg] = e
        tot = functools.reduce(jnp.add, _chunks(e), tot)
    inv = 1.0 / _lanes_same_head(tot, jnp.add)
    return en * inv, _tile_lanes(inv, PAGE_FLAT)


def _fox_sample_body(pt_ref, q_ref, kn_ref, vn_ref, lfn_ref, *refs, n_q, n_pages):
    del pt_ref
    n = PAGES_PER_STEP
    k_refs, v_refs, lf_refs = refs[:n], refs[n:2 * n], refs[2 * n:3 * n]
    o_ref, s_ref, carry_ref, pn_ref, inv_ref, acc_ref = refs[3 * n:]
    step = pl.program_id(1)
    k_steps = n_pages // n
    q_all = (q_ref[...] * ATTN_SCALE).astype(BF16)
    lane_pg = lax.broadcasted_iota(jnp.int32, (V7X_SUBLANES, PAGE_FLAT), 1)
    lane = lax.broadcasted_iota(jnp.int32, (V7X_SUBLANES, V7X_LANES), 1)
    row = lax.broadcasted_iota(jnp.int32, (V7X_SUBLANES, V7X_LANES), 0)

    @pl.when(step == 0)
    def _():
        carry_ref[...] = jnp.zeros(carry_ref.shape, F32)
        acc_ref[...] = jnp.zeros(acc_ref.shape, F32)

    @pl.when(step < k_steps)
    def _():
        for r in range(n):
            c = jnp.broadcast_to(lf_refs[r][...], (V7X_SUBLANES, PAGE_FLAT))
            total = _lanes_same_head(functools.reduce(jnp.add, _chunks(c)), jnp.add)
            shift = N_HEADS
            while shift < PAGE_FLAT:
                c = c + jnp.where(lane_pg >= shift, pltpu.roll(c, shift, 1), 0.0)
                shift *= 2
            c = c + _tile_lanes(carry_ref[...], PAGE_FLAT)
            carry_ref[...] = carry_ref[...] + total
            k2 = k_refs[r][...].reshape(PAGE_FLAT, HEAD_DIM).astype(BF16)
            s_ref[step * n + r] = _diag_scores(q_all, k2, n_q) - c

    @pl.when(step == k_steps - 1)
    def _():
        kn = _pad_rows(kn_ref[...], V7X_LANES).astype(BF16)
        cn = jnp.broadcast_to(lfn_ref[...], (V7X_SUBLANES, V7X_LANES))
        shift = N_HEADS
        while shift < n_q * N_HEADS:
            cn = cn + jnp.where(lane >= shift, pltpu.roll(cn, shift, 1), 0.0)
            shift *= 2
        sn = _diag_scores(q_all, kn, n_q) - (cn + carry_ref[...])
        sn = jnp.where((lane < n_q * N_HEADS) & (lane // N_HEADS <= row), sn, NEG_INF)
        pn, inv = _sample_softmax(s_ref, sn, n_pages)
        pn_ref[...] = pn
        inv_ref[...] = inv

    @pl.when(step >= k_steps)
    def _():
        for r in range(n):
            p = s_ref[(step - k_steps) * n + r] * inv_ref[...]
            v2 = v_refs[r][...].reshape(PAGE_FLAT, HEAD_DIM).astype(BF16)
            acc_ref[...] += jnp.dot(_expand_probs(p, n_q), v2, preferred_element_type=F32)

    @pl.when(step == 2 * k_steps - 1)
    def _():
        vn = _pad_rows(vn_ref[...], V7X_LANES).astype(BF16)
        o_ref[...] = acc_ref[...] + jnp.dot(_expand_probs(pn_ref[...], n_q), vn, preferred_element_type=F32)


def _moba_sample_body(pt_ref, q_ref, kn_ref, vn_ref, slope_ref, *refs, n_q, n_pages, past_len):
    del pt_ref
    n = PAGES_PER_STEP
    k_refs, v_refs = refs[:n], refs[n:2 * n]
    o_ref, s_ref, gate_ref, pn_ref, inv_ref, acc_ref = refs[2 * n:]
    step = pl.program_id(1)
    k_steps = n_pages // n
    pages_per_block = MOBA_BLOCK // PAGE_SIZE
    n_blocks = n_pages // pages_per_block
    q_all = q_ref[...].astype(BF16)
    lane = lax.broadcasted_iota(jnp.int32, (V7X_SUBLANES, V7X_LANES), 1)
    row = lax.broadcasted_iota(jnp.int32, (V7X_SUBLANES, V7X_LANES), 0)

    @pl.when(step == 0)
    def _():
        gate_ref[...] = jnp.zeros(gate_ref.shape, F32)
        acc_ref[...] = jnp.zeros(acc_ref.shape, F32)

    @pl.when(step < k_steps)
    def _():
        for r in range(n):
            k2 = k_refs[r][...].reshape(PAGE_FLAT, HEAD_DIM).astype(BF16)
            raw = _diag_scores(q_all, k2, n_q)
            s_ref[step * n + r] = raw
            blk = (step * n + r) // pages_per_block
            gate_ref[blk] = gate_ref[blk] + functools.reduce(jnp.add, _chunks(raw))

    @pl.when(step == k_steps - 1)
    def _():
        gates = [_lanes_same_head(gate_ref[j], jnp.add) * (1.0 / MOBA_BLOCK) for j in range(n_blocks)]
        slope = slope_ref[...]
        tok = (lax.broadcasted_iota(jnp.int32, (V7X_SUBLANES, PAGE_FLAT), 1) // N_HEADS).astype(F32)
        qpos = (past_len + lax.broadcasted_iota(jnp.int32, (V7X_SUBLANES, PAGE_FLAT), 0)).astype(F32)
        for j in range(n_blocks):
            rank = jnp.zeros(gates[j].shape, jnp.int32)
            for jp in range(n_blocks):
                if jp == j:
                    continue
                beats = (gates[jp] > gates[j]) | ((gates[jp] == gates[j]) & (jp < j))
                rank = rank + beats.astype(jnp.int32)
            sel = _tile_lanes(jnp.where(rank < MOBA_TOPK, 1.0, 0.0), PAGE_FLAT) > 0.5
            for pg in range(j * pages_per_block, (j + 1) * pages_per_block):
                dist = qpos - (tok + float(pg * PAGE_SIZE))
                s_ref[pg] = jnp.where(sel, s_ref[pg] * ATTN_SCALE - slope * dist, NEG_INF)
        kn = _pad_rows(kn_ref[...], V7X_LANES).astype(BF16)
        dist_n = (row - lane // N_HEADS).astype(F32)
        sn = _diag_scores(q_all, kn, n_q) * ATTN_SCALE - slope[:, :V7X_LANES] * dist_n
        sn = jnp.where((lane < n_q * N_HEADS) & (lane // N_HEADS <= row), sn, NEG_INF)
        pn, inv = _sample_softmax(s_ref, sn, n_pages)
        pn_ref[...] = pn
        inv_ref[...] = inv

    @pl.when(step >= k_steps)
    def _():
        for r in range(n):
            p = s_ref[(step - k_steps) * n + r] * inv_ref[...]
            v2 = v_refs[r][...].reshape(PAGE_FLAT, HEAD_DIM).astype(BF16)
            acc_ref[...] += jnp.dot(_expand_probs(p, n_q), v2, preferred_element_type=F32)

    @pl.when(step == 2 * k_steps - 1)
    def _():
        vn = _pad_rows(vn_ref[...], V7X_LANES).astype(BF16)
        o_ref[...] = acc_ref[...] + jnp.dot(_expand_probs(pn_ref[...], n_q), vn, preferred_element_type=F32)


def _page_spec(block, slot, k_phase, r, k_steps):
    n = PAGES_PER_STEP
    zeros = (0,) * (len(block) - 2)

    def index_map(b, s, pt):
        sk = jnp.minimum(s, k_steps - 1) if k_phase else jnp.maximum(s - k_steps, 0)
        return (slot, pt[b, sk * n + r]) + zeros

    return pl.BlockSpec(block, index_map)


def _sample_attention(kind, page_table, q, k_new, v_new, cache_k, cache_v, slot, *, lf_new=None,
                      cache_lf=None, slopes_flat=None, past_len=None):
    db, qh, hd = q.shape
    n_q = qh // N_HEADS
    n_pages = page_table.shape[1]
    n = PAGES_PER_STEP
    k_steps = n_pages // n
    page_block = (None, None, PAGE_SIZE, N_HEADS, HEAD_DIM)
    per_b = lambda b, s, pt: (b, 0, 0)
    in_specs = [pl.BlockSpec((None, qh, hd), per_b)] * 3
    args = [q, k_new, v_new]
    if kind == "fox":
        in_specs.append(pl.BlockSpec((None, 1, V7X_LANES), per_b))
        args.append(lf_new)
    else:
        in_specs.append(pl.BlockSpec((1, PAGE_FLAT), lambda b, s, pt: (0, 0)))
        args.append(slopes_flat)
    in_specs += [_page_spec(page_block, slot, True, r, k_steps) for r in range(n)]
    in_specs += [_page_spec(page_block, slot, False, r, k_steps) for r in range(n)]
    args += [cache_k] * n + [cache_v] * n
    state = pltpu.VMEM((V7X_SUBLANES, V7X_LANES), F32)
    if kind == "fox":
        in_specs += [_page_spec((None, None, 1, PAGE_FLAT), slot, True, r, k_steps) for r in range(n)]
        args += [cache_lf] * n
        body = functools.partial(_fox_sample_body, n_q=n_q, n_pages=n_pages)
        mid_scratch = [state]
    else:
        body = functools.partial(_moba_sample_body, n_q=n_q, n_pages=n_pages, past_len=past_len)
        mid_scratch = [pltpu.VMEM((n_pages * PAGE_SIZE // MOBA_BLOCK, V7X_SUBLANES, V7X_LANES), F32)]
    page_vmem = PAGE_SIZE * N_HEADS * V7X_LANES * 4
    block_bytes = 2 * n * page_vmem + 4 * qh * V7X_LANES * 4 + n * V7X_SUBLANES * PAGE_FLAT * 4
    scratch_bytes = n_pages * V7X_SUBLANES * PAGE_FLAT * 4 + 4 * page_vmem
    return pl.pallas_call(
        body,
        grid_spec=pltpu.PrefetchScalarGridSpec(
            num_scalar_prefetch=1,
            grid=(db, 2 * k_steps),
            in_specs=in_specs,
            out_specs=pl.BlockSpec((None, qh, hd), per_b),
            scratch_shapes=[pltpu.VMEM((n_pages, V7X_SUBLANES, PAGE_FLAT), F32)] + mid_scratch + [
                state,
                pltpu.VMEM((V7X_SUBLANES, PAGE_FLAT), F32),
                pltpu.VMEM((qh, hd), F32),
            ],
        ),
        out_shape=jax.ShapeDtypeStruct((db, qh, hd), F32),
        compiler_params=pltpu.CompilerParams(
            dimension_semantics=("arbitrary", "arbitrary"),
            vmem_limit_bytes=_vmem_limit(block_bytes, scratch_bytes)),
        name=kind + "_sample",
    )(page_table, *args)


def kernel(x_prompt, x_sample, cache_k_fox, cache_v_fox, cache_logf_fox, cache_k_moba, cache_v_moba, state_pool, page_table, p_prompt, p_sample, norm_mix, w_in_fox, b_forget, w_in_moba, w_in_pool, w_pool_group, pool_scale, w_out, w_ple, norm_ple, w_ple_gate, norm_final):
    batch, seq, d = x_prompt.shape
    dec_b, dec_t, _ = x_sample.shape
    depth = norm_mix.shape[0]
    n_pages = page_table.shape[1]
    past_len = n_pages * PAGE_SIZE
    n_p, n_s = batch * seq, dec_b * dec_t
    assert d == D_MODEL and n_p % ROW_TILE == 0 and n_s % ROW_TILE == 0 and seq % MOBA_BLOCK == 0
    assert past_len % MOBA_BLOCK == 0 and n_pages % PAGES_PER_STEP == 0 and dec_t * N_HEADS <= V7X_LANES

    hp = x_prompt.reshape(n_p, d)
    hs = x_sample.reshape(n_s, d)
    pp = p_prompt.reshape(depth, n_p, -1)
    ps = p_sample.reshape(depth, n_s, -1)
    slopes = 2.0 ** (-8.0 * jnp.arange(1, N_HEADS + 1, dtype=F32) / N_HEADS)
    slopes_flat = jnp.tile(slopes, PAGE_SIZE)[None, :]
    lf_cache_flat = cache_logf_fox.reshape(cache_logf_fox.shape[:2] + (1, PAGE_FLAT))

    def heads(t2d, lead):
        return t2d.reshape(lead + (N_HEADS, HEAD_DIM))

    def sample_rows(t2d):
        return t2d.reshape(dec_b, dec_t * N_HEADS, HEAD_DIM)

    outs = {k: [] for k in ("kf_p", "vf_p", "lf_p", "km_p", "vm_p", "pl_p",
                            "kf_s", "vf_s", "lf_s", "km_s", "vm_s", "pl_s")}
    for i in range(depth):
        kind, slot = i % N_MIXERS, i // N_MIXERS
        g_mix = norm_mix[i][None, :]
        if kind == 0:
            w_main = w_in_fox[slot][:, :4 * BRANCH].astype(BF16)
            wf = jnp.pad(w_in_fox[slot][:, 4 * BRANCH:], ((0, 0), (0, V7X_LANES - N_HEADS))).astype(BF16)
            bf = jnp.pad(b_forget[slot], (0, V7X_LANES - N_HEADS))[None, :]
            proj_p, lf_p = _norm_proj(hp, g_mix, w_main, wf, bf)
            proj_s, lf_s = _norm_proj(hs, g_mix, w_main, wf, bf)
            ct = _fox_decay(lf_p.reshape(batch, seq, V7X_LANES))
            a_p = _fox_prompt(proj_p.reshape(batch, seq, 4 * BRANCH), ct).reshape(n_p, BRANCH)
            lf_new = jnp.pad(lf_s[:, :N_HEADS].reshape(dec_b, 1, dec_t * N_HEADS),
                             ((0, 0), (0, 0), (0, V7X_LANES - dec_t * N_HEADS)))
            a_s = _sample_attention(
                "fox", page_table, sample_rows(proj_s[:, :BRANCH]), sample_rows(proj_s[:, BRANCH:2 * BRANCH]),
                sample_rows(proj_s[:, 2 * BRANCH:3 * BRANCH]), cache_k_fox, cache_v_fox, slot,
                lf_new=lf_new, cache_lf=lf_cache_flat).reshape(n_s, BRANCH)
            outs["kf_p"].append(heads(proj_p[:, BRANCH:2 * BRANCH], (batch, seq)))
            outs["vf_p"].append(heads(proj_p[:, 2 * BRANCH:3 * BRANCH], (batch, seq)))
            outs["lf_p"].append(lf_p[:, :N_HEADS].reshape(batch, seq, N_HEADS))
            outs["kf_s"].append(heads(proj_s[:, BRANCH:2 * BRANCH], (dec_b, dec_t)))
            outs["vf_s"].append(heads(proj_s[:, 2 * BRANCH:3 * BRANCH], (dec_b, dec_t)))
            outs["lf_s"].append(lf_s[:, :N_HEADS].reshape(dec_b, dec_t, N_HEADS))
            z_col = 3
        elif kind == 1:
            w_main = w_in_moba[slot].astype(BF16)
            proj_p = _norm_proj(hp, g_mix, w_main)
            proj_s = _norm_proj(hs, g_mix, w_main)
            a_p = _moba_prompt(proj_p.reshape(batch, seq, 4 * BRANCH), slopes).reshape(n_p, BRANCH)
            a_s = _sample_attention(
                "moba", page_table, sample_rows(proj_s[:, :BRANCH]), sample_rows(proj_s[:, BRANCH:2 * BRANCH]),
                sample_rows(proj_s[:, 2 * BRANCH:3 * BRANCH]), cache_k_moba, cache_v_moba, slot,
                slopes_flat=slopes_flat, past_len=past_len).reshape(n_s, BRANCH)
            outs["km_p"].append(heads(proj_p[:, BRANCH:2 * BRANCH], (batch, seq)))
            outs["vm_p"].append(heads(proj_p[:, 2 * BRANCH:3 * BRANCH], (batch, seq)))
            outs["km_s"].append(heads(proj_s[:, BRANCH:2 * BRANCH], (dec_b, dec_t)))
            outs["vm_s"].append(heads(proj_s[:, 2 * BRANCH:3 * BRANCH], (dec_b, dec_t)))
            z_col = 3
        else:
            w_main = w_in_pool[slot].astype(BF16)
            wg = w_pool_group[slot].astype(BF16)
            sc = pool_scale[slot][None, :]
            proj_p = _norm_proj(hp, g_mix, w_main)
            proj_s = _norm_proj(hs, g_mix, w_main)
            a_p, hist_p = _pool(proj_p.reshape(batch, seq, 2 * BRANCH), None, wg, sc, 0)
            a_s, hist_s = _pool(proj_s.reshape(dec_b, dec_t, 2 * BRANCH), state_pool[slot], wg, sc, past_len)
            a_p, a_s = a_p.reshape(n_p, BRANCH), a_s.reshape(n_s, BRANCH)
            outs["pl_p"].append(hist_p)
            outs["pl_s"].append(hist_s)
            z_col = 1
        g_final = norm_final[None, :] if i == depth - 1 else None
        w_o, w_p, w_g = w_out[i].astype(BF16), w_ple[i].astype(BF16), w_ple_gate[i].astype(BF16)
        g_ple = norm_ple[i][None, :]
        hp = _post(a_p, proj_p, z_col, hp, pp[i], w_o, w_p, w_g, g_ple, g_final)
        hs = _post(a_s, proj_s, z_col, hs, ps[i], w_o, w_p, w_g, g_ple, g_final)

    st = lambda key: jnp.stack(outs[key])
    return (hp.reshape(batch, seq, d), hs.reshape(dec_b, dec_t, d),
            st("kf_p"), st("vf_p"), st("lf_p"), st("km_p"), st("vm_p"), st("pl_p"),
            st("kf_s"), st("vf_s"), st("lf_s"), st("km_s"), st("vm_s"), st("pl_s"))
```

```python
import functools

import jax
import jax.numpy as jnp
from jax import lax
from jax.experimental import pallas as pl
from jax.experimental.pallas import tpu as pltpu

F32 = jnp.float32
BF16 = jnp.bfloat16

D_MODEL = 1024
N_HEADS = 16
HEAD_DIM = 64
BRANCH = N_HEADS * HEAD_DIM
PAGE_SIZE = 128
MOBA_BLOCK = 256
MOBA_TOPK = 3
POOL_WINDOWS = (2, 4, 8, 16)
POOL_GROUP_W = BRANCH // len(POOL_WINDOWS)
POOL_HIST = max(POOL_WINDOWS) - 1
N_MIXERS = 3
RMS_EPS = 1e-6
NEG_INF = -1e30
ATTN_SCALE = HEAD_DIM ** -0.5

V7X_LANES = 128
V7X_SUBLANES = 8
V7X_VMEM_BYTES = 64 * 1024 * 1024
VMEM_CEILING = V7X_VMEM_BYTES - 8 * 1024 * 1024

HEADS_PER_VREG = V7X_LANES // HEAD_DIM
N_HEAD_PAIRS = N_HEADS // HEADS_PER_VREG
PAGES_PER_STEP = 8
ROW_TILE = 512
ATTN_TILE = MOBA_BLOCK


def _vmem_limit(block_bytes, scratch_bytes=0):
    est = 2 * block_bytes + scratch_bytes
    return int(min(VMEM_CEILING, max(32 * 1024 * 1024, est * 3 // 2)))


def _nt_dot(a, b):
    return lax.dot_general(a, b, (((1,), (1,)), ((), ())), preferred_element_type=F32)


def _rmsnorm(x, g):
    return x * lax.rsqrt(jnp.mean(x * x, axis=-1, keepdims=True) + RMS_EPS) * g


def _split_bf16(x):
    hi = x.astype(BF16)
    lo = (x - hi.astype(F32)).astype(BF16)
    return hi, lo


def _norm_proj_body(*refs, has_forget):
    if has_forget:
        x_ref, g_ref, w_ref, wf_ref, bf_ref, o_ref, lf_ref, xn_ref = refs
    else:
        x_ref, g_ref, w_ref, o_ref, xn_ref = refs

    @pl.when(pl.program_id(1) == 0)
    def _():
        xn = _rmsnorm(x_ref[...], g_ref[...]).astype(BF16)
        xn_ref[...] = xn
        if has_forget:
            raw = jnp.dot(xn, wf_ref[...], preferred_element_type=F32) + bf_ref[...]
            lf_ref[...] = jax.nn.log_sigmoid(raw)

    o_ref[...] = jnp.dot(xn_ref[...], w_ref[...], preferred_element_type=F32)


def _norm_proj(x, g, w, wf=None, bf=None):
    n, d = x.shape
    n_out = w.shape[1]
    tm = ROW_TILE
    tn = min(n_out, 2048)
    has_forget = wf is not None
    in_specs = [
        pl.BlockSpec((tm, d), lambda i, j: (i, 0)),
        pl.BlockSpec((1, d), lambda i, j: (0, 0)),
        pl.BlockSpec((d, tn), lambda i, j: (0, j)),
    ]
    out_shape = [jax.ShapeDtypeStruct((n, n_out), F32)]
    out_specs = [pl.BlockSpec((tm, tn), lambda i, j: (i, j))]
    args = [x, g, w]
    block_bytes = tm * d * 4 + d * tn * 2 + tm * tn * 4
    if has_forget:
        in_specs += [pl.BlockSpec((d, V7X_LANES), lambda i, j: (0, 0)),
                     pl.BlockSpec((1, V7X_LANES), lambda i, j: (0, 0))]
        out_shape.append(jax.ShapeDtypeStruct((n, V7X_LANES), F32))
        out_specs.append(pl.BlockSpec((tm, V7X_LANES), lambda i, j: (i, 0)))
        args += [wf, bf]
        block_bytes += d * V7X_LANES * 2 + tm * V7X_LANES * 4
    res = pl.pallas_call(
        functools.partial(_norm_proj_body, has_forget=has_forget),
        grid=(n // tm, n_out // tn),
        in_specs=in_specs,
        out_specs=out_specs,
        out_shape=out_shape,
        scratch_shapes=[pltpu.VMEM((tm, d), BF16)],
        compiler_params=pltpu.CompilerParams(
            dimension_semantics=("arbitrary", "arbitrary"),
            vmem_limit_bytes=_vmem_limit(block_bytes, tm * d * 2)),
        name="norm_proj_forget" if has_forget else "norm_proj",
    )(*args)
    return res if has_forget else res[0]


def _post_body(*refs, final_norm):
    if final_norm:
        (a_ref, z_ref, h_ref, p_ref, wo_ref, wp_ref, wg_ref, g_ref, gf_ref, o_ref) = refs
    else:
        (a_ref, z_ref, h_ref, p_ref, wo_ref, wp_ref, wg_ref, g_ref, o_ref) = refs
    gated = (a_ref[...] * jax.nn.silu(z_ref[...])).astype(BF16)
    h1 = h_ref[...] + jnp.dot(gated, wo_ref[...], preferred_element_type=F32)
    hn = _rmsnorm(h1, g_ref[...]).astype(BF16)
    gate = jax.nn.sigmoid(jnp.dot(hn, wg_ref[...], preferred_element_type=F32))
    pp = jnp.dot(p_ref[...].astype(BF16), wp_ref[...], preferred_element_type=F32)
    h2 = h1 + pp * gate
    if final_norm:
        h2 = _rmsnorm(h2, gf_ref[...])
    o_ref[...] = h2


def _post(a, proj, z_col, h, p, w_out, w_ple, w_gate, g_ple, g_final=None):
    n, d = h.shape
    tm = ROW_TILE
    ple = p.shape[1]
    final_norm = g_final is not None
    row = lambda i: (i, 0)
    const = lambda i: (0, 0)
    in_specs = [
        pl.BlockSpec((tm, BRANCH), row),
        pl.BlockSpec((tm, BRANCH), lambda i: (i, z_col)),
        pl.BlockSpec((tm, d), row),
        pl.BlockSpec((tm, ple), row),
        pl.BlockSpec((BRANCH, d), const),
        pl.BlockSpec((ple, d), const),
        pl.BlockSpec((d, d), const),
        pl.BlockSpec((1, d), const),
    ]
    args = [a, proj, h, p, w_out, w_ple, w_gate, g_ple]
    if final_norm:
        in_specs.append(pl.BlockSpec((1, d), const))
        args.append(g_final)
    block_bytes = (3 * tm * d + tm * ple + tm * d) * 4 + (BRANCH * d + ple * d + d * d) * 2
    return pl.pallas_call(
        functools.partial(_post_body, final_norm=final_norm),
        grid=(n // tm,),
        in_specs=in_specs,
        out_specs=pl.BlockSpec((tm, d), row),
        out_shape=jax.ShapeDtypeStruct((n, d), F32),
        compiler_params=pltpu.CompilerParams(
            dimension_semantics=("arbitrary",),
            vmem_limit_bytes=_vmem_limit(block_bytes)),
        name="post_final" if final_norm else "post",
    )(*args)


def _fox_decay_body(lf_ref, c_ref):
    seq = lf_ref.shape[0]
    x = lf_ref[...].T[:N_HEADS, :]
    lane = lax.broadcasted_iota(jnp.int32, x.shape, 1)
    shift = 1
    while shift < seq:
        x = x + jnp.where(lane >= shift, pltpu.roll(x, shift, 1), 0.0)
        shift *= 2
    x = jnp.concatenate([x, jnp.zeros((V7X_LANES - N_HEADS, seq), F32)], axis=0)
    c_ref[...] = x.T


def _fox_decay(lf):
    b, s, _ = lf.shape
    return pl.pallas_call(
        _fox_decay_body,
        grid=(b,),
        in_specs=[pl.BlockSpec((None, s, V7X_LANES), lambda i: (i, 0, 0))],
        out_specs=pl.BlockSpec((None, s, V7X_LANES), lambda i: (i, 0, 0)),
        out_shape=jax.ShapeDtypeStruct((b, s, V7X_LANES), F32),
        compiler_params=pltpu.CompilerParams(
            dimension_semantics=("arbitrary",),
            vmem_limit_bytes=_vmem_limit(s * V7X_LANES * 4 * 2, s * V7X_LANES * 4 * 4)),
        name="fox_decay",
    )(lf)


def _prompt_attn_body(*refs, kind):
    if kind == "fox":
        (q_ref, k_ref, v_ref, c_ref, o_ref,
         kb_ref, vt_ref, qm_ref, acc_ref, m_ref, l_ref, bias_ref) = refs
    else:
        (slope_ref, q_ref, k_ref, v_ref, o_ref,
         kb_ref, vt_ref, qm_ref, acc_ref, m_ref, l_ref, sel_ref, kmean_ref) = refs
    hp = pl.program_id(1)
    n_tiles, tile, _ = kb_ref.shape
    seq = n_tiles * tile

    lane = lax.broadcasted_iota(jnp.int32, (tile, V7X_LANES), 1)
    in_head = (lane < HEAD_DIM, lane >= HEAD_DIM)
    if kind == "moba":
        kmean_ref[...] = jnp.zeros(kmean_ref.shape, F32)
    for j in range(n_tiles):
        rows = pl.ds(j * tile, tile)
        kf = k_ref[rows, :]
        kb_ref[j] = kf.astype(BF16)
        vt_ref[j] = v_ref[rows, :].T.astype(BF16)
        q = q_ref[rows, :]
        for i in range(HEADS_PER_VREG):
            qm_ref[i, rows, :] = (jnp.where(in_head[i], q, 0.0) * ATTN_SCALE).astype(BF16)
        if kind == "moba":
            kmean_ref[pl.ds(j, 1), :] = jnp.mean(kf, axis=0, keepdims=True)

    if kind == "fox":
        src = lax.broadcasted_iota(jnp.int32, (V7X_LANES, V7X_LANES), 0)
        for j in range(n_tiles):
            c = c_ref[pl.ds(j * tile, tile), :]
            c_hi, c_rest = _split_bf16(c)
            c_mid, c_lo = _split_bf16(c - c_hi.astype(F32))
            del c_rest
            for i in range(HEADS_PER_VREG):
                pick = jnp.where(src == HEADS_PER_VREG * hp + i, 1.0, 0.0).astype(BF16)
                dot = lambda x: jnp.dot(x, pick, preferred_element_type=F32)
                bias_ref[i, j] = dot(c_hi) + (dot(c_mid) + dot(c_lo))
    else:
        km_hi, km_lo = _split_bf16(kmean_ref[...])
        jrow = lax.broadcasted_iota(jnp.int32, (kmean_ref.shape[0], tile), 0)
        for j in range(n_tiles):
            q = q_ref[pl.ds(j * tile, tile), :]
            valid = jrow < j
            for i in range(HEADS_PER_VREG):
                q_hi, q_lo = _split_bf16(jnp.where(in_head[i], q, 0.0))
                gate = _nt_dot(km_hi, q_hi) + (_nt_dot(km_hi, q_lo) + _nt_dot(km_lo, q_hi))
                gate = jnp.where(valid, gate, NEG_INF)
                rank = jnp.zeros(gate.shape, jnp.int32)
                for jp in range(j):
                    gj = gate[jp:jp + 1, :]
                    beats = (gj > gate) | ((gj == gate) & (jp < jrow))
                    rank = rank + beats.astype(jnp.int32)
                sel_ref[i, :, pl.ds(j * tile, tile)] = jnp.where(valid & (rank < MOBA_TOPK), 1.0, 0.0)

    kloc = lax.broadcasted_iota(jnp.int32, (tile, tile), 0)
    qloc = lax.broadcasted_iota(jnp.int32, (tile, tile), 1)
    for kj in range(n_tiles):
        q0 = kj * tile
        width = seq - q0
        queries = pl.ds(q0, width)
        for i in range(HEADS_PER_VREG):
            st = _nt_dot(kb_ref[kj], qm_ref[i, queries, :])
            if kind == "fox":
                st = st - jnp.concatenate([bias_ref[i, kj]] * (width // V7X_LANES), axis=1)
            else:
                krow = lax.broadcasted_iota(jnp.int32, (tile, width), 0)
                qcol = lax.broadcasted_iota(jnp.int32, (tile, width), 1)
                st = st - slope_ref[HEADS_PER_VREG * hp + i] * (qcol - krow).astype(F32)
            own = jnp.where(kloc <= qloc, st[:, :tile], NEG_INF)
            if width == tile:
                st = own
            elif kind == "fox":
                st = jnp.concatenate([own, st[:, tile:]], axis=1)
            else:
                picked = sel_ref[i, pl.ds(kj, 1), pl.ds(q0 + tile, width - tile)] > 0.5
                st = jnp.concatenate([own, jnp.where(picked, st[:, tile:], NEG_INF)], axis=1)
            m_new = jnp.max(st, axis=0, keepdims=True)
            if kj > 0:
                m_old = m_ref[i, :, queries]
                m_new = jnp.maximum(m_old, m_new)
            p = jnp.exp(st - m_new)
            if kind == "moba" and width > tile:
                p = jnp.concatenate([p[:, :tile], jnp.where(picked, p[:, tile:], 0.0)], axis=1)
            l_new = jnp.sum(p, axis=0, keepdims=True)
            pv = jnp.dot(vt_ref[kj], p.astype(BF16), preferred_element_type=F32)
            if kj > 0:
                alpha = jnp.exp(m_old - m_new)
                l_new = alpha * l_ref[i, :, queries] + l_new
                pv = alpha * acc_ref[i, :, queries] + pv
            m_ref[i, :, queries] = m_new
            l_ref[i, :, queries] = l_new
            acc_ref[i, :, queries] = pv

    row = lax.broadcasted_iota(jnp.int32, (V7X_LANES, tile), 0)
    for j in range(n_tiles):
        cols = pl.ds(j * tile, tile)
        out_t = jnp.where(row < HEAD_DIM, acc_ref[0, :, cols] / l_ref[0, :, cols],
                          acc_ref[1, :, cols] / l_ref[1, :, cols])
        o_ref[cols, :] = out_t.T


def _prompt_attention(kind, proj, aux):
    b, s, _ = proj.shape
    tile = ATTN_TILE
    n_tiles = s // tile
    cols = BRANCH // V7X_LANES
    seq_block = lambda col0: pl.BlockSpec((None, s, V7X_LANES), lambda bi, hp: (bi, 0, col0 + hp))
    in_specs = [seq_block(0), seq_block(cols), seq_block(2 * cols)]
    args = [proj, proj, proj]
    state = lambda rows: pltpu.VMEM((HEADS_PER_VREG, rows, s), F32)
    scratch = [
        pltpu.VMEM((n_tiles, tile, V7X_LANES), BF16),
        pltpu.VMEM((n_tiles, V7X_LANES, tile), BF16),
        pltpu.VMEM((HEADS_PER_VREG, s, V7X_LANES), BF16),
        state(V7X_LANES), state(1), state(1),
    ]
    if kind == "fox":
        in_specs.append(pl.BlockSpec((None, s, V7X_LANES), lambda bi, hp: (bi, 0, 0)))
        args.append(aux)
        scratch.append(pltpu.VMEM((HEADS_PER_VREG, n_tiles, tile, V7X_LANES), F32))
    else:
        in_specs.insert(0, pl.BlockSpec(memory_space=pltpu.SMEM))
        args.insert(0, aux)
        scratch += [state(2 * V7X_SUBLANES), pltpu.VMEM((2 * V7X_SUBLANES, V7X_LANES), F32)]
    block_bytes = 5 * s * V7X_LANES * 4
    scratch_bytes = 20 * s * V7X_LANES * 4
    return pl.pallas_call(
        functools.partial(_prompt_attn_body, kind=kind),
        grid=(b, N_HEAD_PAIRS),
        in_specs=in_specs,
        out_specs=pl.BlockSpec((None, s, V7X_LANES), lambda bi, hp: (bi, 0, hp)),
        out_shape=jax.ShapeDtypeStruct((b, s, BRANCH), F32),
        scratch_shapes=scratch,
        compiler_params=pltpu.CompilerParams(
            dimension_semantics=("arbitrary", "arbitrary"),
            vmem_limit_bytes=_vmem_limit(block_bytes, scratch_bytes)),
        name=kind + "_prompt",
    )(*args)


def _pool_body(*refs, start_pos, has_hist):
    if has_hist:
        u_ref, hist_ref, wg_ref, sc_ref, y_ref, ho_ref, ext_ref = refs
    else:
        u_ref, wg_ref, sc_ref, y_ref, ho_ref, ext_ref = refs
    t = pl.program_id(1)
    ts = u_ref.shape[0]
    pad = POOL_HIST + 1

    @pl.when(t == 0)
    def _():
        if has_hist:
            ext_ref[0:1, :] = jnp.zeros((1, BRANCH), F32)
            ext_ref[1:pad, :] = hist_ref[...]
        else:
            ext_ref[0:pad, :] = jnp.zeros((pad, BRANCH), F32)

    @pl.when(t > 0)
    def _():
        ext_ref[0:pad, :] = ext_ref[ts:ts + pad, :]

    u = u_ref[...]
    ext_ref[pad:pad + ts, :] = u
    pos = start_pos + t * ts + lax.broadcasted_iota(jnp.int32, (ts, 1), 0)
    for gi, w in enumerate(POOL_WINDOWS):
        c0 = gi * POOL_GROUP_W
        win = ext_ref[pl.ds(pad, ts), pl.ds(c0, POOL_GROUP_W)]
        for back in range(1, w):
            win = win + ext_ref[pl.ds(pad - back, ts), pl.ds(c0, POOL_GROUP_W)]
        cnt = jnp.minimum(pos + 1, w).astype(F32)
        d = win / cnt - u[:, c0:c0 + POOL_GROUP_W]
        y = jnp.dot(d.astype(BF16), wg_ref[gi], preferred_element_type=F32)
        y_ref[:, c0:c0 + POOL_GROUP_W] = y * sc_ref[:, c0:c0 + POOL_GROUP_W]

    @pl.when(t == pl.num_programs(1) - 1)
    def _():
        ho_ref[...] = ext_ref[ts + 1:ts + pad, :]


def _pool(proj, hist, w_group, scale, start_pos):
    b, t, _ = proj.shape
    ts = min(t, ROW_TILE)
    has_hist = hist is not None
    pad = POOL_HIST + 1
    in_specs = [pl.BlockSpec((None, ts, BRANCH), lambda bi, ti: (bi, ti, 0))]
    args = [proj]
    if has_hist:
        in_specs.append(pl.BlockSpec((None, POOL_HIST, BRANCH), lambda bi, ti: (bi, 0, 0)))
        args.append(hist)
    in_specs += [
        pl.BlockSpec((len(POOL_WINDOWS), POOL_GROUP_W, POOL_GROUP_W), lambda bi, ti: (0, 0, 0)),
        pl.BlockSpec((1, BRANCH), lambda bi, ti: (0, 0)),
    ]
    args += [w_group, scale]
    block_bytes = (2 * ts * BRANCH + 2 * pad * BRANCH) * 4 + w_group.size * 2
    return pl.pallas_call(
        functools.partial(_pool_body, start_pos=start_pos, has_hist=has_hist),
        grid=(b, t // ts),
        in_specs=in_specs,
        out_specs=[
            pl.BlockSpec((None, ts, BRANCH), lambda bi, ti: (bi, ti, 0)),
            pl.BlockSpec((None, POOL_HIST, BRANCH), lambda bi, ti: (bi, 0, 0)),
        ],
        out_shape=[
            jax.ShapeDtypeStruct((b, t, BRANCH), F32),
            jax.ShapeDtypeStruct((b, POOL_HIST, BRANCH), F32),
        ],
        scratch_shapes=[pltpu.VMEM((ts + pad, BRANCH), F32)],
        compiler_params=pltpu.CompilerParams(
            dimension_semantics=("arbitrary", "arbitrary"),
            vmem_limit_bytes=_vmem_limit(block_bytes, (ts + pad) * BRANCH * 4)),
        name="pool_hist" if has_hist else "pool",
    )(*args)


def _lane_cumsum(x):
    lane = lax.broadcasted_iota(jnp.int32, x.shape, 1)
    shift = 1
    while shift < x.shape[1]:
        x = x + jnp.where(lane >= shift, pltpu.roll(x, shift, 1), 0.0)
        shift *= 2
    return x


def _lane_total(x):
    shift = 1
    while shift < x.shape[1]:
        x = x + pltpu.roll(x, shift, 1)
        shift *= 2
    return x


def _page_matrix(block):
    return block.reshape(BRANCH, PAGE_SIZE).astype(BF16)


def _new_token_page(x):
    x = jnp.concatenate([x, jnp.zeros((PAGE_SIZE - x.shape[0], x.shape[1]), F32)], axis=0)
    return x.T.astype(BF16)


def _sample_attn_body(pt_ref, q_ref, kn_ref, vn_ref, aux_ref, *refs, kind, n_q, n_pages, past_len):
    del pt_ref
    n = PAGES_PER_STEP
    k_refs, v_refs, rest = refs[:n], refs[n:2 * n], refs[2 * n:]
    if kind == "fox":
        lf_refs, rest = rest[:n], rest[n:]
    o_ref, qbd_ref, s_ref, side_ref, l_ref, acc_ref = rest
    step = pl.program_id(1)
    k_steps = n_pages // n
    rows = n_q * N_HEADS
    row = lax.broadcasted_iota(jnp.int32, (rows, PAGE_SIZE), 0)
    lane = lax.broadcasted_iota(jnp.int32, (rows, PAGE_SIZE), 1)
    t_row = row // N_HEADS

    @pl.when(step == 0)
    def _():
        q = q_ref[...]
        if kind == "fox":
            q = q * ATTN_SCALE
        d_of_col = lax.broadcasted_iota(jnp.int32, (HEAD_DIM, BRANCH), 1) % HEAD_DIM
        spread = jnp.where(lax.broadcasted_iota(jnp.int32, (HEAD_DIM, BRANCH), 0) == d_of_col, 1.0, 0.0)
        q_all = jnp.dot(q.astype(BF16), spread.astype(BF16), preferred_element_type=F32)
        r = lax.broadcasted_iota(jnp.int32, (rows, BRANCH), 0)
        c = lax.broadcasted_iota(jnp.int32, (rows, BRANCH), 1)
        qbd_ref[...] = jnp.where(c // HEAD_DIM == r % N_HEADS, q_all, 0.0).astype(BF16)
        acc_ref[...] = jnp.zeros(acc_ref.shape, F32)
        if kind == "fox":
            side_ref[...] = jnp.zeros(side_ref.shape, F32)

    @pl.when(step < k_steps)
    def _():
        for r in range(n):
            pg = step * n + r
            raw = jnp.dot(qbd_ref[...], _page_matrix(k_refs[r][...]), preferred_element_type=F32)
            if kind == "fox":
                lf = lf_refs[r][...]
                c = _lane_cumsum(lf) + side_ref[...]
                side_ref[...] = side_ref[...] + _lane_total(lf)
                s_ref[pg] = raw - jnp.concatenate([c] * n_q, axis=0)
            else:
                s_ref[pg] = raw
                side_ref[pg] = jnp.sum(raw, axis=1, keepdims=True)

    @pl.when(step == k_steps - 1)
    def _():
        raw_n = jnp.dot(qbd_ref[...], _new_token_page(kn_ref[...]), preferred_element_type=F32)
        causal = (lane < n_q) & (lane <= t_row)
        if kind == "fox":
            c_n = _lane_cumsum(aux_ref[...]) + side_ref[...]
            s_n = raw_n - jnp.concatenate([c_n] * n_q, axis=0)
        else:
            slope = aux_ref[...]
            per_block = MOBA_BLOCK // PAGE_SIZE
            n_blocks = n_pages // per_block
            gates = []
            for j in range(n_blocks):
                tot = functools.reduce(jnp.add, [side_ref[j * per_block + e] for e in range(per_block)])
                gates.append(tot * (1.0 / MOBA_BLOCK))
            for j in range(n_blocks):
                rank = jnp.zeros(gates[j].shape, jnp.int32)
                for jp in range(n_blocks):
                    if jp != j:
                        beats = (gates[jp] > gates[j]) | ((gates[jp] == gates[j]) & (jp < j))
                        rank = rank + beats.astype(jnp.int32)
                sel = rank < MOBA_TOPK
                for pg in range(j * per_block, (j + 1) * per_block):
                    dist = (past_len - pg * PAGE_SIZE + t_row - lane).astype(F32)
                    s_ref[pg] = jnp.where(sel, s_ref[pg] * ATTN_SCALE - slope * dist, NEG_INF)
            s_n = raw_n * ATTN_SCALE - slope * (t_row - lane).astype(F32)
        s_ref[n_pages] = jnp.where(causal, s_n, NEG_INF)
        m = functools.reduce(jnp.maximum, [s_ref[pg] for pg in range(n_pages + 1)])
        m = jnp.max(m, axis=1, keepdims=True)
        tot = jnp.zeros((rows, PAGE_SIZE), F32)
        for pg in range(n_pages + 1):
            e = jnp.exp(s_ref[pg] - m)
            s_ref[pg] = e
            tot = tot + e
        l_ref[...] = jnp.sum(tot, axis=1, keepdims=True)

    @pl.when(step >= k_steps)
    def _():
        for r in range(n):
            p = s_ref[(step - k_steps) * n + r].astype(BF16)
            acc_ref[...] += _nt_dot(p, _page_matrix(v_refs[r][...]))

    @pl.when(step == 2 * k_steps - 1)
    def _():
        acc = acc_ref[...] + _nt_dot(s_ref[n_pages].astype(BF16), _new_token_page(vn_ref[...]))
        head = lax.broadcasted_iota(jnp.int32, (rows, V7X_LANES), 0) % N_HEADS
        picked = jnp.zeros((rows, V7X_LANES), F32)
        for j in range(N_HEAD_PAIRS):
            picked = jnp.where(head // HEADS_PER_VREG == j, acc[:, j * V7X_LANES:(j + 1) * V7X_LANES], picked)
        picked = jnp.where(head % HEADS_PER_VREG == 0, picked, pltpu.roll(picked, HEAD_DIM, 1))
        o_ref[...] = picked[:, :HEAD_DIM] / l_ref[...]


def _page_spec(block, slot, k_phase, r, k_steps):
    n = PAGES_PER_STEP
    zeros = (0,) * (len(block) - 2)

    def index_map(b, s, pt):
        sk = jnp.minimum(s, k_steps - 1) if k_phase else jnp.maximum(s - k_steps, 0)
        return (slot, pt[b, sk * n + r]) + zeros

    return pl.BlockSpec(block, index_map)


def _sample_attention(kind, page_table, q, k_new, v_new, aux, cache_k, cache_v, slot, *, cache_lf=None,
                      past_len=None):
    db, rows, hd = q.shape
    n_q = rows // N_HEADS
    n_pages = page_table.shape[1]
    n = PAGES_PER_STEP
    k_steps = n_pages // n
    page_block = (None, None, N_HEADS, HEAD_DIM, PAGE_SIZE)
    per_b = lambda b, s, pt: (b, 0, 0)
    in_specs = [pl.BlockSpec((None, rows, hd), per_b),
                pl.BlockSpec((None,) + k_new.shape[1:], per_b),
                pl.BlockSpec((None,) + v_new.shape[1:], per_b)]
    if kind == "fox":
        in_specs.append(pl.BlockSpec((None, N_HEADS, PAGE_SIZE), per_b))
        side = pltpu.VMEM((N_HEADS, PAGE_SIZE), F32)
    else:
        in_specs.append(pl.BlockSpec((rows, PAGE_SIZE), lambda b, s, pt: (0, 0)))
        side = pltpu.VMEM((n_pages, rows, 1), F32)
    in_specs += [_page_spec(page_block, slot, True, r, k_steps) for r in range(n)]
    in_specs += [_page_spec(page_block, slot, False, r, k_steps) for r in range(n)]
    args = [q, k_new, v_new, aux] + [cache_k] * n + [cache_v] * n
    if kind == "fox":
        in_specs += [_page_spec((None, None, N_HEADS, PAGE_SIZE), slot, True, r, k_steps) for r in range(n)]
        args += [cache_lf] * n
    page_bytes = BRANCH * PAGE_SIZE * 4
    block_bytes = 2 * n * page_bytes + 4 * rows * BRANCH * 4
    scratch_bytes = (n_pages + 1) * rows * PAGE_SIZE * 4 + rows * BRANCH * 8 + 6 * page_bytes
    return pl.pallas_call(
        functools.partial(_sample_attn_body, kind=kind, n_q=n_q, n_pages=n_pages, past_len=past_len),
        grid_spec=pltpu.PrefetchScalarGridSpec(
            num_scalar_prefetch=1,
            grid=(db, 2 * k_steps),
            in_specs=in_specs,
            out_specs=pl.BlockSpec((None, rows, hd), per_b),
            scratch_shapes=[
                pltpu.VMEM((rows, BRANCH), BF16),
                pltpu.VMEM((n_pages + 1, rows, PAGE_SIZE), F32),
                side,
                pltpu.VMEM((rows, 1), F32),
                pltpu.VMEM((rows, BRANCH), F32),
            ],
        ),
        out_shape=jax.ShapeDtypeStruct((db, rows, hd), F32),
        compiler_params=pltpu.CompilerParams(
            dimension_semantics=("arbitrary", "arbitrary"),
            vmem_limit_bytes=_vmem_limit(block_bytes, scratch_bytes)),
        name=kind + "_sample",
    )(page_table, *args)


def kernel(x_prompt, x_sample, cache_k_fox, cache_v_fox, cache_logf_fox, cache_k_moba, cache_v_moba, state_pool, page_table, p_prompt, p_sample, norm_mix, w_in_fox, b_forget, w_in_moba, w_in_pool, w_pool_group, pool_scale, w_out, w_ple, norm_ple, w_ple_gate, norm_final):
    batch, seq, d = x_prompt.shape
    dec_b, dec_t, _ = x_sample.shape
    depth = norm_mix.shape[0]
    n_pages = page_table.shape[1]
    past_len = n_pages * PAGE_SIZE
    n_p, n_s = batch * seq, dec_b * dec_t
    assert d == D_MODEL and n_p % ROW_TILE == 0 and n_s % ROW_TILE == 0 and seq % MOBA_BLOCK == 0
    assert past_len % MOBA_BLOCK == 0 and n_pages % PAGES_PER_STEP == 0 and dec_t <= V7X_SUBLANES

    hp = x_prompt.reshape(n_p, d)
    hs = x_sample.reshape(n_s, d)
    pp = p_prompt.reshape(depth, n_p, -1)
    ps = p_sample.reshape(depth, n_s, -1)
    slopes = 2.0 ** (-8.0 * jnp.arange(1, N_HEADS + 1, dtype=F32) / N_HEADS)
    slope_rows = jnp.broadcast_to(jnp.tile(slopes, dec_t)[:, None], (dec_t * N_HEADS, PAGE_SIZE))
    tok_minor = lambda c: jnp.transpose(c, (0, 1, 3, 4, 2))
    kf_cache, vf_cache = tok_minor(cache_k_fox), tok_minor(cache_v_fox)
    km_cache, vm_cache = tok_minor(cache_k_moba), tok_minor(cache_v_moba)
    lf_cache = jnp.transpose(cache_logf_fox, (0, 1, 3, 2))

    def heads(t2d, lead):
        return t2d.reshape(lead + (N_HEADS, HEAD_DIM))

    def sample_q(t2d):
        return t2d.reshape(dec_b, dec_t * N_HEADS, HEAD_DIM)

    def sample_new(t2d):
        return jnp.pad(t2d.reshape(dec_b, dec_t, BRANCH), ((0, 0), (0, V7X_SUBLANES - dec_t), (0, 0)))

    outs = {k: [] for k in ("kf_p", "vf_p", "lf_p", "km_p", "vm_p", "pl_p",
                            "kf_s", "vf_s", "lf_s", "km_s", "vm_s", "pl_s")}
    for i in range(depth):
        kind, slot = i % N_MIXERS, i // N_MIXERS
        g_mix = norm_mix[i][None, :]
        if kind == 0:
            w_main = w_in_fox[slot][:, :4 * BRANCH].astype(BF16)
            wf = jnp.pad(w_in_fox[slot][:, 4 * BRANCH:], ((0, 0), (0, V7X_LANES - N_HEADS))).astype(BF16)
            bf = jnp.pad(b_forget[slot], (0, V7X_LANES - N_HEADS))[None, :]
            proj_p, lf_p = _norm_proj(hp, g_mix, w_main, wf, bf)
            proj_s, lf_s = _norm_proj(hs, g_mix, w_main, wf, bf)
            c_p = _fox_decay(lf_p.reshape(batch, seq, V7X_LANES))
            a_p = _prompt_attention("fox", proj_p.reshape(batch, seq, 4 * BRANCH), c_p).reshape(n_p, BRANCH)
            lf_new = jnp.pad(jnp.transpose(lf_s[:, :N_HEADS].reshape(dec_b, dec_t, N_HEADS), (0, 2, 1)),
                             ((0, 0), (0, 0), (0, PAGE_SIZE - dec_t)))
            a_s = _sample_attention(
                "fox", page_table, sample_q(proj_s[:, :BRANCH]), sample_new(proj_s[:, BRANCH:2 * BRANCH]),
                sample_new(proj_s[:, 2 * BRANCH:3 * BRANCH]), lf_new, kf_cache, vf_cache, slot,
                cache_lf=lf_cache).reshape(n_s, BRANCH)
            outs["kf_p"].append(heads(proj_p[:, BRANCH:2 * BRANCH], (batch, seq)))
            outs["vf_p"].append(heads(proj_p[:, 2 * BRANCH:3 * BRANCH], (batch, seq)))
            outs["lf_p"].append(lf_p[:, :N_HEADS].reshape(batch, seq, N_HEADS))
            outs["kf_s"].append(heads(proj_s[:, BRANCH:2 * BRANCH], (dec_b, dec_t)))
            outs["vf_s"].append(heads(proj_s[:, 2 * BRANCH:3 * BRANCH], (dec_b, dec_t)))
            outs["lf_s"].append(lf_s[:, :N_HEADS].reshape(dec_b, dec_t, N_HEADS))
            z_col = 3
        elif kind == 1:
            w_main = w_in_moba[slot].astype(BF16)
            proj_p = _norm_proj(hp, g_mix, w_main)
            proj_s = _norm_proj(hs, g_mix, w_main)
            a_p = _prompt_attention("moba", proj_p.reshape(batch, seq, 4 * BRANCH), slopes).reshape(n_p, BRANCH)
            a_s = _sample_attention(
                "moba", page_table, sample_q(proj_s[:, :BRANCH]), sample_new(proj_s[:, BRANCH:2 * BRANCH]),
                sample_new(proj_s[:, 2 * BRANCH:3 * BRANCH]), slope_rows, km_cache, vm_cache, slot,
                past_len=past_len).reshape(n_s, BRANCH)
            outs["km_p"].append(heads(proj_p[:, BRANCH:2 * BRANCH], (batch, seq)))
            outs["vm_p"].append(heads(proj_p[:, 2 * BRANCH:3 * BRANCH], (batch, seq)))
            outs["km_s"].append(heads(proj_s[:, BRANCH:2 * BRANCH], (dec_b, dec_t)))
            outs["vm_s"].append(heads(proj_s[:, 2 * BRANCH:3 * BRANCH], (dec_b, dec_t)))
            z_col = 3
        else:
            w_main = w_in_pool[slot].astype(BF16)
            wg = w_pool_group[slot].astype(BF16)
            sc = pool_scale[slot][None, :]
            proj_p = _norm_proj(hp, g_mix, w_main)
            proj_s = _norm_proj(hs, g_mix, w_main)
            a_p, hist_p = _pool(proj_p.reshape(batch, seq, 2 * BRANCH), None, wg, sc, 0)
            a_s, hist_s = _pool(proj_s.reshape(dec_b, dec_t, 2 * BRANCH), state_pool[slot], wg, sc, past_len)
            a_p, a_s = a_p.reshape(n_p, BRANCH), a_s.reshape(n_s, BRANCH)
            outs["pl_p"].append(hist_p)
            outs["pl_s"].append(hist_s)
            z_col = 1
        g_final = norm_final[None, :] if i == depth - 1 else None
        w_o, w_p, w_g = w_out[i].astype(BF16), w_ple[i].astype(BF16), w_ple_gate[i].astype(BF16)
        g_ple = norm_ple[i][None, :]
        hp = _post(a_p, proj_p, z_col, hp, pp[i], w_o, w_p, w_g, g_ple, g_final)
        hs = _post(a_s, proj_s, z_col, hs, ps[i], w_o, w_p, w_g, g_ple, g_final)

    st = lambda key: jnp.stack(outs[key])
    return (hp.reshape(batch, seq, d), hs.reshape(dec_b, dec_t, d),
            st("kf_p"), st("vf_p"), st("lf_p"), st("km_p"), st("vm_p"), st("pl_p"),
            st("kf_s"), st("vf_s"), st("lf_s"), st("km_s"), st("vm_s"), st("pl_s"))
```

```python
import functools

import jax
import jax.numpy as jnp
from jax import lax
from jax.experimental import pallas as pl
from jax.experimental.pallas import tpu as pltpu

F32 = jnp.float32
BF16 = jnp.bfloat16

D_MODEL = 1024
N_HEADS = 16
HEAD_DIM = 64
BRANCH = N_HEADS * HEAD_DIM
PAGE_SIZE = 128
MOBA_BLOCK = 256
MOBA_TOPK = 3
POOL_WINDOWS = (2, 4, 8, 16)
POOL_GROUP_W = BRANCH // len(POOL_WINDOWS)
POOL_HIST = max(POOL_WINDOWS) - 1
N_MIXERS = 3
RMS_EPS = 1e-6
NEG_INF = -1e30
ATTN_SCALE = HEAD_DIM ** -0.5

V7X_LANES = 128
V7X_SUBLANES = 8
V7X_VMEM_BYTES = 64 * 1024 * 1024
VMEM_CEILING = V7X_VMEM_BYTES - 8 * 1024 * 1024

HEADS_PER_VREG = V7X_LANES // HEAD_DIM
N_HEAD_PAIRS = N_HEADS // HEADS_PER_VREG
PAGES_PER_STEP = 16
ROW_TILE = 512
ATTN_TILE = MOBA_BLOCK


def _vmem_limit(block_bytes, scratch_bytes=0):
    est = 2 * block_bytes + scratch_bytes
    return int(min(VMEM_CEILING, max(32 * 1024 * 1024, est * 3 // 2)))


def _nt_dot(a, b):
    return lax.dot_general(a, b, (((1,), (1,)), ((), ())), preferred_element_type=F32)


def _rmsnorm(x, g):
    return x * lax.rsqrt(jnp.mean(x * x, axis=-1, keepdims=True) + RMS_EPS) * g


def _split_bf16(x):
    hi = x.astype(BF16)
    lo = (x - hi.astype(F32)).astype(BF16)
    return hi, lo


def _norm_proj_body(*refs, has_forget):
    if has_forget:
        x_ref, g_ref, w_ref, wf_ref, bf_ref, o_ref, lf_ref, xn_ref = refs
    else:
        x_ref, g_ref, w_ref, o_ref, xn_ref = refs

    @pl.when(pl.program_id(1) == 0)
    def _():
        xn = _rmsnorm(x_ref[...], g_ref[...]).astype(BF16)
        xn_ref[...] = xn
        if has_forget:
            raw = jnp.dot(xn, wf_ref[...], preferred_element_type=F32) + bf_ref[...]
            lf_ref[...] = jax.nn.log_sigmoid(raw)

    o_ref[...] = jnp.dot(xn_ref[...], w_ref[...], preferred_element_type=F32)


def _norm_proj(x, g, w, wf=None, bf=None):
    n, d = x.shape
    n_out = w.shape[1]
    tm = ROW_TILE
    tn = min(n_out, 2048)
    has_forget = wf is not None
    in_specs = [
        pl.BlockSpec((tm, d), lambda i, j: (i, 0)),
        pl.BlockSpec((1, d), lambda i, j: (0, 0)),
        pl.BlockSpec((d, tn), lambda i, j: (0, j)),
    ]
    out_shape = [jax.ShapeDtypeStruct((n, n_out), F32)]
    out_specs = [pl.BlockSpec((tm, tn), lambda i, j: (i, j))]
    args = [x, g, w]
    block_bytes = tm * d * 4 + d * tn * 2 + tm * tn * 4
    if has_forget:
        in_specs += [pl.BlockSpec((d, V7X_LANES), lambda i, j: (0, 0)),
                     pl.BlockSpec((1, V7X_LANES), lambda i, j: (0, 0))]
        out_shape.append(jax.ShapeDtypeStruct((n, V7X_LANES), F32))
        out_specs.append(pl.BlockSpec((tm, V7X_LANES), lambda i, j: (i, 0)))
        args += [wf, bf]
        block_bytes += d * V7X_LANES * 2 + tm * V7X_LANES * 4
    res = pl.pallas_call(
        functools.partial(_norm_proj_body, has_forget=has_forget),
        grid=(n // tm, n_out // tn),
        in_specs=in_specs,
        out_specs=out_specs,
        out_shape=out_shape,
        scratch_shapes=[pltpu.VMEM((tm, d), BF16)],
        compiler_params=pltpu.CompilerParams(
            dimension_semantics=("arbitrary", "arbitrary"),
            vmem_limit_bytes=_vmem_limit(block_bytes, tm * d * 2)),
        name="norm_proj_forget" if has_forget else "norm_proj",
    )(*args)
    return res if has_forget else res[0]


def _post_body(*refs, final_norm):
    if final_norm:
        (a_ref, z_ref, h_ref, p_ref, wo_ref, wp_ref, wg_ref, g_ref, gf_ref, o_ref) = refs
    else:
        (a_ref, z_ref, h_ref, p_ref, wo_ref, wp_ref, wg_ref, g_ref, o_ref) = refs
    gated = (a_ref[...] * jax.nn.silu(z_ref[...])).astype(BF16)
    h1 = h_ref[...] + jnp.dot(gated, wo_ref[...], preferred_element_type=F32)
    hn = _rmsnorm(h1, g_ref[...]).astype(BF16)
    gate = jax.nn.sigmoid(jnp.dot(hn, wg_ref[...], preferred_element_type=F32))
    pp = jnp.dot(p_ref[...].astype(BF16), wp_ref[...], preferred_element_type=F32)
    h2 = h1 + pp * gate
    if final_norm:
        h2 = _rmsnorm(h2, gf_ref[...])
    o_ref[...] = h2


def _post(a, proj, z_col, h, p, layer, w_out, w_ple, w_gate, g_ple, g_final=None):
    n, d = h.shape
    tm = ROW_TILE
    ple = p.shape[2]
    final_norm = g_final is not None
    row = lambda i: (i, 0)
    const = lambda i: (0, 0)
    in_specs = [
        pl.BlockSpec((tm, BRANCH), row),
        pl.BlockSpec((tm, BRANCH), lambda i: (i, z_col)),
        pl.BlockSpec((tm, d), row),
        pl.BlockSpec((None, tm, ple), lambda i: (layer, i, 0)),
        pl.BlockSpec((BRANCH, d), const),
        pl.BlockSpec((ple, d), const),
        pl.BlockSpec((d, d), const),
        pl.BlockSpec((1, d), const),
    ]
    args = [a, proj, h, p, w_out, w_ple, w_gate, g_ple]
    if final_norm:
        in_specs.append(pl.BlockSpec((1, d), const))
        args.append(g_final)
    block_bytes = (3 * tm * d + tm * ple + tm * d) * 4 + (BRANCH * d + ple * d + d * d) * 2
    return pl.pallas_call(
        functools.partial(_post_body, final_norm=final_norm),
        grid=(n // tm,),
        in_specs=in_specs,
        out_specs=pl.BlockSpec((tm, d), row),
        out_shape=jax.ShapeDtypeStruct((n, d), F32),
        compiler_params=pltpu.CompilerParams(
            dimension_semantics=("arbitrary",),
            vmem_limit_bytes=_vmem_limit(block_bytes)),
        name="post_final" if final_norm else "post",
    )(*args)


def _fox_decay_body(lf_ref, c_ref):
    seq = lf_ref.shape[0]
    x = lf_ref[...].T[:N_HEADS, :]
    lane = lax.broadcasted_iota(jnp.int32, x.shape, 1)
    shift = 1
    while shift < seq:
        x = x + jnp.where(lane >= shift, pltpu.roll(x, shift, 1), 0.0)
        shift *= 2
    x = jnp.concatenate([x, jnp.zeros((V7X_LANES - N_HEADS, seq), F32)], axis=0)
    c_ref[...] = x.T


def _fox_decay(lf):
    b, s, _ = lf.shape
    return pl.pallas_call(
        _fox_decay_body,
        grid=(b,),
        in_specs=[pl.BlockSpec((None, s, V7X_LANES), lambda i: (i, 0, 0))],
        out_specs=pl.BlockSpec((None, s, V7X_LANES), lambda i: (i, 0, 0)),
        out_shape=jax.ShapeDtypeStruct((b, s, V7X_LANES), F32),
        compiler_params=pltpu.CompilerParams(
            dimension_semantics=("arbitrary",),
            vmem_limit_bytes=_vmem_limit(s * V7X_LANES * 4 * 2, s * V7X_LANES * 4 * 4)),
        name="fox_decay",
    )(lf)


def _prompt_attn_body(*refs, kind):
    if kind == "fox":
        (q_ref, k_ref, v_ref, c_ref, o_ref, kt_out_ref, vt_out_ref,
         kb_ref, vt_ref, qm_ref, acc_ref, m_ref, l_ref, bias_ref) = refs
    else:
        (slope_ref, q_ref, k_ref, v_ref, o_ref, kt_out_ref, vt_out_ref,
         kb_ref, vt_ref, qm_ref, acc_ref, m_ref, l_ref, sel_ref, kmean_ref, bias_ref) = refs
    hp = pl.program_id(1)
    n_tiles, tile, _ = kb_ref.shape
    seq = n_tiles * tile

    lane = lax.broadcasted_iota(jnp.int32, (tile, V7X_LANES), 1)
    in_head = (lane < HEAD_DIM, lane >= HEAD_DIM)
    if kind == "moba":
        kmean_ref[...] = jnp.zeros(kmean_ref.shape, F32)
    for j in range(n_tiles):
        rows = pl.ds(j * tile, tile)
        kf = k_ref[rows, :]
        kb_ref[j] = kf.astype(BF16)
        kt_out_ref[:, rows] = kf.T
        vt = v_ref[rows, :].T
        vt_out_ref[:, rows] = vt
        vt_ref[j] = vt.astype(BF16)
        q = q_ref[rows, :]
        for i in range(HEADS_PER_VREG):
            qm_ref[i, rows, :] = (jnp.where(in_head[i], q, 0.0) * ATTN_SCALE).astype(BF16)
        if kind == "moba":
            kmean_ref[pl.ds(j, 1), :] = jnp.mean(kf, axis=0, keepdims=True)

    if kind == "fox":
        src = lax.broadcasted_iota(jnp.int32, (V7X_LANES, V7X_LANES), 0)
        for j in range(n_tiles):
            c = c_ref[pl.ds(j * tile, tile), :]
            c_hi, c_rest = _split_bf16(c)
            c_mid, c_lo = _split_bf16(c - c_hi.astype(F32))
            del c_rest
            for i in range(HEADS_PER_VREG):
                pick = jnp.where(src == HEADS_PER_VREG * hp + i, 1.0, 0.0).astype(BF16)
                dot = lambda x: jnp.dot(x, pick, preferred_element_type=F32)
                bias_ref[i, j] = dot(c_hi) + (dot(c_mid) + dot(c_lo))
    else:
        km_hi, km_lo = _split_bf16(kmean_ref[...])
        jrow = lax.broadcasted_iota(jnp.int32, (kmean_ref.shape[0], tile), 0)
        for j in range(n_tiles):
            q = q_ref[pl.ds(j * tile, tile), :]
            valid = jrow < j
            for i in range(HEADS_PER_VREG):
                q_hi, q_lo = _split_bf16(jnp.where(in_head[i], q, 0.0))
                gate = _nt_dot(km_hi, q_hi) + (_nt_dot(km_hi, q_lo) + _nt_dot(km_lo, q_hi))
                gate = jnp.where(valid, gate, NEG_INF)
                rank = jnp.zeros(gate.shape, jnp.int32)
                for jp in range(j):
                    gj = gate[jp:jp + 1, :]
                    beats = (gj > gate) | ((gj == gate) & (jp < jrow))
                    rank = rank + beats.astype(jnp.int32)
                sel_ref[i, :, pl.ds(j * tile, tile)] = jnp.where(valid & (rank < MOBA_TOPK), 1.0, 0.0)
        rel = (lax.broadcasted_iota(jnp.int32, (tile, seq), 1)
               - lax.broadcasted_iota(jnp.int32, (tile, seq), 0)).astype(F32)
        for i in range(HEADS_PER_VREG):
            bias_ref[i] = slope_ref[HEADS_PER_VREG * hp + i] * rel

    kloc = lax.broadcasted_iota(jnp.int32, (tile, tile), 0)
    qloc = lax.broadcasted_iota(jnp.int32, (tile, tile), 1)
    for kj in range(n_tiles):
        q0 = kj * tile
        width = seq - q0
        queries = pl.ds(q0, width)
        for i in range(HEADS_PER_VREG):
            st = _nt_dot(kb_ref[kj], qm_ref[i, queries, :])
            if kind == "fox":
                st = st - jnp.concatenate([bias_ref[i, kj]] * (width // V7X_LANES), axis=1)
            else:
                st = st - bias_ref[i, :, pl.ds(0, width)]
            own = jnp.where(kloc <= qloc, st[:, :tile], NEG_INF)
            if width == tile:
                st = own
            elif kind == "fox":
                st = jnp.concatenate([own, st[:, tile:]], axis=1)
            else:
                picked = sel_ref[i, pl.ds(kj, 1), pl.ds(q0 + tile, width - tile)] > 0.5
                st = jnp.concatenate([own, jnp.where(picked, st[:, tile:], NEG_INF)], axis=1)
            m_new = jnp.max(st, axis=0, keepdims=True)
            if kj > 0:
                m_old = m_ref[i, :, queries]
                m_new = jnp.maximum(m_old, m_new)
            p = jnp.exp(st - m_new)
            if kind == "moba" and width > tile:
                p = jnp.concatenate([p[:, :tile], jnp.where(picked, p[:, tile:], 0.0)], axis=1)
            l_new = jnp.sum(p, axis=0, keepdims=True)
            pv = jnp.dot(vt_ref[kj], p.astype(BF16), preferred_element_type=F32)
            if kj > 0:
                alpha = jnp.exp(m_old - m_new)
                l_new = alpha * l_ref[i, :, queries] + l_new
                pv = alpha * acc_ref[i, :, queries] + pv
            m_ref[i, :, queries] = m_new
            l_ref[i, :, queries] = l_new
            acc_ref[i, :, queries] = pv

    row = lax.broadcasted_iota(jnp.int32, (V7X_LANES, tile), 0)
    for j in range(n_tiles):
        cols = pl.ds(j * tile, tile)
        out_t = jnp.where(row < HEAD_DIM, acc_ref[0, :, cols] / l_ref[0, :, cols],
                          acc_ref[1, :, cols] / l_ref[1, :, cols])
        o_ref[cols, :] = out_t.T


def _prompt_attention(kind, proj, aux):
    b, s, _ = proj.shape
    tile = ATTN_TILE
    n_tiles = s // tile
    cols = BRANCH // V7X_LANES
    seq_block = lambda col0: pl.BlockSpec((None, s, V7X_LANES), lambda bi, hp: (bi, 0, col0 + hp))
    in_specs = [seq_block(0), seq_block(cols), seq_block(2 * cols)]
    args = [proj, proj, proj]
    state = lambda rows: pltpu.VMEM((HEADS_PER_VREG, rows, s), F32)
    scratch = [
        pltpu.VMEM((n_tiles, tile, V7X_LANES), BF16),
        pltpu.VMEM((n_tiles, V7X_LANES, tile), BF16),
        pltpu.VMEM((HEADS_PER_VREG, s, V7X_LANES), BF16),
        state(V7X_LANES), state(1), state(1),
    ]
    if kind == "fox":
        in_specs.append(pl.BlockSpec((None, s, V7X_LANES), lambda bi, hp: (bi, 0, 0)))
        args.append(aux)
        scratch.append(pltpu.VMEM((HEADS_PER_VREG, n_tiles, tile, V7X_LANES), F32))
    else:
        in_specs.insert(0, pl.BlockSpec(memory_space=pltpu.SMEM))
        args.insert(0, aux)
        scratch += [state(2 * V7X_SUBLANES), pltpu.VMEM((2 * V7X_SUBLANES, V7X_LANES), F32),
                    pltpu.VMEM((HEADS_PER_VREG, tile, s), F32)]
    block_bytes = 7 * s * V7X_LANES * 4
    scratch_bytes = 24 * s * V7X_LANES * 4
    transposed = pl.BlockSpec((None, V7X_LANES, s), lambda bi, hp: (bi, hp, 0))
    return pl.pallas_call(
        functools.partial(_prompt_attn_body, kind=kind),
        grid=(b, N_HEAD_PAIRS),
        in_specs=in_specs,
        out_specs=[pl.BlockSpec((None, s, V7X_LANES), lambda bi, hp: (bi, 0, hp)), transposed, transposed],
        out_shape=[jax.ShapeDtypeStruct((b, s, BRANCH), F32),
                   jax.ShapeDtypeStruct((b, BRANCH, s), F32),
                   jax.ShapeDtypeStruct((b, BRANCH, s), F32)],
        scratch_shapes=scratch,
        compiler_params=pltpu.CompilerParams(
            dimension_semantics=("arbitrary", "arbitrary"),
            vmem_limit_bytes=_vmem_limit(block_bytes, scratch_bytes)),
        name=kind + "_prompt",
    )(*args)


def _pool_body(*refs, start_pos, has_hist):
    if has_hist:
        u_ref, hist_ref, wg_ref, sc_ref, y_ref, ho_ref, ext_ref = refs
    else:
        u_ref, wg_ref, sc_ref, y_ref, ho_ref, ext_ref = refs
    t = pl.program_id(1)
    ts = u_ref.shape[0]
    pad = POOL_HIST + 1

    @pl.when(t == 0)
    def _():
        if has_hist:
            ext_ref[0:1, :] = jnp.zeros((1, BRANCH), F32)
            ext_ref[1:pad, :] = hist_ref[...]
        else:
            ext_ref[0:pad, :] = jnp.zeros((pad, BRANCH), F32)

    @pl.when(t > 0)
    def _():
        ext_ref[0:pad, :] = ext_ref[ts:ts + pad, :]

    u = u_ref[...]
    ext_ref[pad:pad + ts, :] = u
    pos = start_pos + t * ts + lax.broadcasted_iota(jnp.int32, (ts, 1), 0)
    for gi, w in enumerate(POOL_WINDOWS):
        c0 = gi * POOL_GROUP_W
        win = ext_ref[pl.ds(pad, ts), pl.ds(c0, POOL_GROUP_W)]
        for back in range(1, w):
            win = win + ext_ref[pl.ds(pad - back, ts), pl.ds(c0, POOL_GROUP_W)]
        cnt = jnp.minimum(pos + 1, w).astype(F32)
        d = win / cnt - u[:, c0:c0 + POOL_GROUP_W]
        y = jnp.dot(d.astype(BF16), wg_ref[gi], preferred_element_type=F32)
        y_ref[:, c0:c0 + POOL_GROUP_W] = y * sc_ref[:, c0:c0 + POOL_GROUP_W]

    @pl.when(t == pl.num_programs(1) - 1)
    def _():
        ho_ref[...] = ext_ref[ts + 1:ts + pad, :]


def _pool(proj, hist, w_group, scale, start_pos):
    b, t, _ = proj.shape
    ts = min(t, ROW_TILE)
    has_hist = hist is not None
    pad = POOL_HIST + 1
    in_specs = [pl.BlockSpec((None, ts, BRANCH), lambda bi, ti: (bi, ti, 0))]
    args = [proj]
    if has_hist:
        in_specs.append(pl.BlockSpec((None, POOL_HIST, BRANCH), lambda bi, ti: (bi, 0, 0)))
        args.append(hist)
    in_specs += [
        pl.BlockSpec((len(POOL_WINDOWS), POOL_GROUP_W, POOL_GROUP_W), lambda bi, ti: (0, 0, 0)),
        pl.BlockSpec((1, BRANCH), lambda bi, ti: (0, 0)),
    ]
    args += [w_group, scale]
    block_bytes = (2 * ts * BRANCH + 2 * pad * BRANCH) * 4 + w_group.size * 2
    return pl.pallas_call(
        functools.partial(_pool_body, start_pos=start_pos, has_hist=has_hist),
        grid=(b, t // ts),
        in_specs=in_specs,
        out_specs=[
            pl.BlockSpec((None, ts, BRANCH), lambda bi, ti: (bi, ti, 0)),
            pl.BlockSpec((None, POOL_HIST, BRANCH), lambda bi, ti: (bi, 0, 0)),
        ],
        out_shape=[
            jax.ShapeDtypeStruct((b, t, BRANCH), F32),
            jax.ShapeDtypeStruct((b, POOL_HIST, BRANCH), F32),
        ],
        scratch_shapes=[pltpu.VMEM((ts + pad, BRANCH), F32)],
        compiler_params=pltpu.CompilerParams(
            dimension_semantics=("arbitrary", "arbitrary"),
            vmem_limit_bytes=_vmem_limit(block_bytes, (ts + pad) * BRANCH * 4)),
        name="pool_hist" if has_hist else "pool",
    )(*args)


def _lane_cumsum(x):
    lane = lax.broadcasted_iota(jnp.int32, x.shape, 1)
    shift = 1
    while shift < x.shape[1]:
        x = x + jnp.where(lane >= shift, pltpu.roll(x, shift, 1), 0.0)
        shift *= 2
    return x


def _lane_total(x):
    shift = 1
    while shift < x.shape[1]:
        x = x + pltpu.roll(x, shift, 1)
        shift *= 2
    return x


def _page_matrix(block):
    return block.reshape(BRANCH, PAGE_SIZE).astype(BF16)


def _new_token_page(x):
    x = jnp.concatenate([x, jnp.zeros((PAGE_SIZE - x.shape[0], x.shape[1]), F32)], axis=0)
    return x.T.astype(BF16)


def _sample_attn_body(pt_ref, q_ref, kn_ref, vn_ref, aux_ref, *refs, kind, n_q, n_pages, past_len):
    del pt_ref
    n = PAGES_PER_STEP
    k_refs, v_refs, rest = refs[:n], refs[n:2 * n], refs[2 * n:]
    if kind == "fox":
        lf_refs, rest = rest[:n], rest[n:]
    o_ref, qbd_ref, s_ref, side_ref, l_ref, acc_ref = rest
    step = pl.program_id(1)
    k_steps = n_pages // n
    rows = n_q * N_HEADS
    row = lax.broadcasted_iota(jnp.int32, (rows, PAGE_SIZE), 0)
    lane = lax.broadcasted_iota(jnp.int32, (rows, PAGE_SIZE), 1)
    t_row = row // N_HEADS

    @pl.when(step == 0)
    def _():
        q = q_ref[...]
        if kind == "fox":
            q = q * ATTN_SCALE
        d_of_col = lax.broadcasted_iota(jnp.int32, (HEAD_DIM, BRANCH), 1) % HEAD_DIM
        spread = jnp.where(lax.broadcasted_iota(jnp.int32, (HEAD_DIM, BRANCH), 0) == d_of_col, 1.0, 0.0)
        q_all = jnp.dot(q.astype(BF16), spread.astype(BF16), preferred_element_type=F32)
        r = lax.broadcasted_iota(jnp.int32, (rows, BRANCH), 0)
        c = lax.broadcasted_iota(jnp.int32, (rows, BRANCH), 1)
        qbd_ref[...] = jnp.where(c // HEAD_DIM == r % N_HEADS, q_all, 0.0).astype(BF16)
        acc_ref[...] = jnp.zeros(acc_ref.shape, F32)
        if kind == "fox":
            side_ref[...] = jnp.zeros(side_ref.shape, F32)

    @pl.when(step < k_steps)
    def _():
        for r in range(n):
            pg = step * n + r
            raw = jnp.dot(qbd_ref[...], _page_matrix(k_refs[r][...]), preferred_element_type=F32)
            if kind == "fox":
                lf = lf_refs[r][...]
                c = _lane_cumsum(lf) + side_ref[...]
                side_ref[...] = side_ref[...] + _lane_total(lf)
                s_ref[pg] = raw - jnp.concatenate([c] * n_q, axis=0)
            else:
                s_ref[pg] = raw
                side_ref[pg] = jnp.sum(raw, axis=1, keepdims=True)

    @pl.when(step == k_steps - 1)
    def _():
        raw_n = jnp.dot(qbd_ref[...], _new_token_page(kn_ref[...]), preferred_element_type=F32)
        causal = (lane < n_q) & (lane <= t_row)
        if kind == "fox":
            c_n = _lane_cumsum(aux_ref[...]) + side_ref[...]
            s_n = raw_n - jnp.concatenate([c_n] * n_q, axis=0)
        else:
            slope = aux_ref[...]
            per_block = MOBA_BLOCK // PAGE_SIZE
            n_blocks = n_pages // per_block
            gates = []
            for j in range(n_blocks):
                tot = functools.reduce(jnp.add, [side_ref[j * per_block + e] for e in range(per_block)])
                gates.append(tot * (1.0 / MOBA_BLOCK))
            for j in range(n_blocks):
                rank = jnp.zeros(gates[j].shape, jnp.int32)
                for jp in range(n_blocks):
                    if jp != j:
                        beats = (gates[jp] > gates[j]) | ((gates[jp] == gates[j]) & (jp < j))
                        rank = rank + beats.astype(jnp.int32)
                sel = rank < MOBA_TOPK
                for pg in range(j * per_block, (j + 1) * per_block):
                    dist = (past_len - pg * PAGE_SIZE + t_row - lane).astype(F32)
                    s_ref[pg] = jnp.where(sel, s_ref[pg] * ATTN_SCALE - slope * dist, NEG_INF)
            s_n = raw_n * ATTN_SCALE - slope * (t_row - lane).astype(F32)
        s_ref[n_pages] = jnp.where(causal, s_n, NEG_INF)
        m = functools.reduce(jnp.maximum, [s_ref[pg] for pg in range(n_pages + 1)])
        m = jnp.max(m, axis=1, keepdims=True)
        tot = jnp.zeros((rows, PAGE_SIZE), F32)
        for pg in range(n_pages + 1):
            e = jnp.exp(s_ref[pg] - m)
            s_ref[pg] = e
            tot = tot + e
        l_ref[...] = jnp.sum(tot, axis=1, keepdims=True)

    @pl.when(step >= k_steps)
    def _():
        for r in range(n):
            p = s_ref[(step - k_steps) * n + r].astype(BF16)
            acc_ref[...] += _nt_dot(p, _page_matrix(v_refs[r][...]))

    @pl.when(step == 2 * k_steps - 1)
    def _():
        acc = acc_ref[...] + _nt_dot(s_ref[n_pages].astype(BF16), _new_token_page(vn_ref[...]))
        head = lax.broadcasted_iota(jnp.int32, (rows, V7X_LANES), 0) % N_HEADS
        picked = jnp.zeros((rows, V7X_LANES), F32)
        for j in range(N_HEAD_PAIRS):
            picked = jnp.where(head // HEADS_PER_VREG == j, acc[:, j * V7X_LANES:(j + 1) * V7X_LANES], picked)
        picked = jnp.where(head % HEADS_PER_VREG == 0, picked, pltpu.roll(picked, HEAD_DIM, 1))
        o_ref[...] = picked[:, :HEAD_DIM] / l_ref[...]


def _page_spec(block, slot, k_phase, r, k_steps):
    n = PAGES_PER_STEP
    zeros = (0,) * (len(block) - 2)

    def index_map(b, s, pt):
        if k_phase:
            row, grp = b, jnp.minimum(s, k_steps - 1)
        else:
            in_phase = s >= k_steps
            row = jnp.where(in_phase, b, jnp.maximum(b - 1, 0))
            grp = jnp.where(in_phase, s - k_steps, k_steps - 1)
        return (slot, pt[row, grp * n + r]) + zeros

    return pl.BlockSpec(block, index_map)


def _sample_attention(kind, page_table, q, k_new, v_new, aux, cache_k, cache_v, slot, *, cache_lf=None,
                      past_len=None):
    db, rows, hd = q.shape
    n_q = rows // N_HEADS
    n_pages = page_table.shape[1]
    n = PAGES_PER_STEP
    k_steps = n_pages // n
    page_block = (None, None, N_HEADS, HEAD_DIM, PAGE_SIZE)
    per_b = lambda b, s, pt: (b, 0, 0)
    in_specs = [pl.BlockSpec((None, rows, hd), per_b),
                pl.BlockSpec((None,) + k_new.shape[1:], per_b),
                pl.BlockSpec((None,) + v_new.shape[1:], per_b)]
    if kind == "fox":
        in_specs.append(pl.BlockSpec((None, N_HEADS, PAGE_SIZE), per_b))
        side = pltpu.VMEM((N_HEADS, PAGE_SIZE), F32)
    else:
        in_specs.append(pl.BlockSpec((rows, PAGE_SIZE), lambda b, s, pt: (0, 0)))
        side = pltpu.VMEM((n_pages, rows, 1), F32)
    in_specs += [_page_spec(page_block, slot, True, r, k_steps) for r in range(n)]
    in_specs += [_page_spec(page_block, slot, False, r, k_steps) for r in range(n)]
    args = [q, k_new, v_new, aux] + [cache_k] * n + [cache_v] * n
    if kind == "fox":
        in_specs += [_page_spec((None, None, N_HEADS, PAGE_SIZE), slot, True, r, k_steps) for r in range(n)]
        args += [cache_lf] * n
    page_bytes = BRANCH * PAGE_SIZE * 4
    block_bytes = 2 * n * page_bytes + 4 * rows * BRANCH * 4
    scratch_bytes = (n_pages + 1) * rows * PAGE_SIZE * 4 + rows * BRANCH * 8 + 6 * page_bytes
    return pl.pallas_call(
        functools.partial(_sample_attn_body, kind=kind, n_q=n_q, n_pages=n_pages, past_len=past_len),
        grid_spec=pltpu.PrefetchScalarGridSpec(
            num_scalar_prefetch=1,
            grid=(db, 2 * k_steps),
            in_specs=in_specs,
            out_specs=pl.BlockSpec((None, rows, hd), per_b),
            scratch_shapes=[
                pltpu.VMEM((rows, BRANCH), BF16),
                pltpu.VMEM((n_pages + 1, rows, PAGE_SIZE), F32),
                side,
                pltpu.VMEM((rows, 1), F32),
                pltpu.VMEM((rows, BRANCH), F32),
            ],
        ),
        out_shape=jax.ShapeDtypeStruct((db, rows, hd), F32),
        compiler_params=pltpu.CompilerParams(
            dimension_semantics=("arbitrary", "arbitrary"),
            vmem_limit_bytes=_vmem_limit(block_bytes, scratch_bytes)),
        name=kind + "_sample",
    )(page_table, *args)


def kernel(x_prompt, x_sample, cache_k_fox, cache_v_fox, cache_logf_fox, cache_k_moba, cache_v_moba, state_pool, page_table, p_prompt, p_sample, norm_mix, w_in_fox, b_forget, w_in_moba, w_in_pool, w_pool_group, pool_scale, w_out, w_ple, norm_ple, w_ple_gate, norm_final):
    batch, seq, d = x_prompt.shape
    dec_b, dec_t, _ = x_sample.shape
    depth = norm_mix.shape[0]
    n_pages = page_table.shape[1]
    past_len = n_pages * PAGE_SIZE
    n_p, n_s = batch * seq, dec_b * dec_t
    assert d == D_MODEL and n_p % ROW_TILE == 0 and n_s % ROW_TILE == 0 and seq % MOBA_BLOCK == 0
    assert past_len % MOBA_BLOCK == 0 and n_pages % PAGES_PER_STEP == 0 and dec_t <= V7X_SUBLANES

    hp = x_prompt.reshape(n_p, d)
    hs = x_sample.reshape(n_s, d)
    pp = p_prompt.reshape(depth, n_p, -1)
    ps = p_sample.reshape(depth, n_s, -1)
    slopes = 2.0 ** (-8.0 * jnp.arange(1, N_HEADS + 1, dtype=F32) / N_HEADS)
    slope_rows = jnp.broadcast_to(jnp.tile(slopes, dec_t)[:, None], (dec_t * N_HEADS, PAGE_SIZE))
    tok_minor = lambda c: jnp.transpose(c, (0, 1, 3, 4, 2))
    kf_cache, vf_cache = tok_minor(cache_k_fox), tok_minor(cache_v_fox)
    km_cache, vm_cache = tok_minor(cache_k_moba), tok_minor(cache_v_moba)
    lf_cache = jnp.transpose(cache_logf_fox, (0, 1, 3, 2))

    def heads(t2d, lead):
        return t2d.reshape(lead + (N_HEADS, HEAD_DIM))

    def sample_q(t2d):
        return t2d.reshape(dec_b, dec_t * N_HEADS, HEAD_DIM)

    def sample_new(t2d):
        return jnp.pad(t2d.reshape(dec_b, dec_t, BRANCH), ((0, 0), (0, V7X_SUBLANES - dec_t), (0, 0)))

    outs = {k: [] for k in ("kf_p", "vf_p", "lf_p", "km_p", "vm_p", "pl_p",
                            "kf_s", "vf_s", "lf_s", "km_s", "vm_s", "pl_s")}
    for i in range(depth):
        kind, slot = i % N_MIXERS, i // N_MIXERS
        g_mix = norm_mix[i][None, :]
        if kind == 0:
            w_main = w_in_fox[slot][:, :4 * BRANCH].astype(BF16)
            wf = jnp.pad(w_in_fox[slot][:, 4 * BRANCH:], ((0, 0), (0, V7X_LANES - N_HEADS))).astype(BF16)
            bf = jnp.pad(b_forget[slot], (0, V7X_LANES - N_HEADS))[None, :]
            proj_p, lf_p = _norm_proj(hp, g_mix, w_main, wf, bf)
            proj_s, lf_s = _norm_proj(hs, g_mix, w_main, wf, bf)
            c_p = _fox_decay(lf_p.reshape(batch, seq, V7X_LANES))
            a_p, kt_p, vt_p = _prompt_attention("fox", proj_p.reshape(batch, seq, 4 * BRANCH), c_p)
            a_p = a_p.reshape(n_p, BRANCH)
            lf_new = jnp.pad(jnp.transpose(lf_s[:, :N_HEADS].reshape(dec_b, dec_t, N_HEADS), (0, 2, 1)),
                             ((0, 0), (0, 0), (0, PAGE_SIZE - dec_t)))
            a_s = _sample_attention(
                "fox", page_table, sample_q(proj_s[:, :BRANCH]), sample_new(proj_s[:, BRANCH:2 * BRANCH]),
                sample_new(proj_s[:, 2 * BRANCH:3 * BRANCH]), lf_new, kf_cache, vf_cache, slot,
                cache_lf=lf_cache).reshape(n_s, BRANCH)
            outs["kf_p"].append(kt_p)
            outs["vf_p"].append(vt_p)
            outs["lf_p"].append(lf_p[:, :N_HEADS].reshape(batch, seq, N_HEADS))
            outs["kf_s"].append(heads(proj_s[:, BRANCH:2 * BRANCH], (dec_b, dec_t)))
            outs["vf_s"].append(heads(proj_s[:, 2 * BRANCH:3 * BRANCH], (dec_b, dec_t)))
            outs["lf_s"].append(lf_s[:, :N_HEADS].reshape(dec_b, dec_t, N_HEADS))
            z_col = 3
        elif kind == 1:
            w_main = w_in_moba[slot].astype(BF16)
            proj_p = _norm_proj(hp, g_mix, w_main)
            proj_s = _norm_proj(hs, g_mix, w_main)
            a_p, kt_p, vt_p = _prompt_attention("moba", proj_p.reshape(batch, seq, 4 * BRANCH), slopes)
            a_p = a_p.reshape(n_p, BRANCH)
            a_s = _sample_attention(
                "moba", page_table, sample_q(proj_s[:, :BRANCH]), sample_new(proj_s[:, BRANCH:2 * BRANCH]),
                sample_new(proj_s[:, 2 * BRANCH:3 * BRANCH]), slope_rows, km_cache, vm_cache, slot,
                past_len=past_len).reshape(n_s, BRANCH)
            outs["km_p"].append(kt_p)
            outs["vm_p"].append(vt_p)
            outs["km_s"].append(heads(proj_s[:, BRANCH:2 * BRANCH], (dec_b, dec_t)))
            outs["vm_s"].append(heads(proj_s[:, 2 * BRANCH:3 * BRANCH], (dec_b, dec_t)))
            z_col = 3
        else:
            w_main = w_in_pool[slot].astype(BF16)
            wg = w_pool_group[slot].astype(BF16)
            sc = pool_scale[slot][None, :]
            proj_p = _norm_proj(hp, g_mix, w_main)
            proj_s = _norm_proj(hs, g_mix, w_main)
            a_p, hist_p = _pool(proj_p.reshape(batch, seq, 2 * BRANCH), None, wg, sc, 0)
            a_s, hist_s = _pool(proj_s.reshape(dec_b, dec_t, 2 * BRANCH), state_pool[slot], wg, sc, past_len)
            a_p, a_s = a_p.reshape(n_p, BRANCH), a_s.reshape(n_s, BRANCH)
            outs["pl_p"].append(hist_p)
            outs["pl_s"].append(hist_s)
            z_col = 1
        g_final = norm_final[None, :] if i == depth - 1 else None
        w_o, w_p, w_g = w_out[i].astype(BF16), w_ple[i].astype(BF16), w_ple_gate[i].astype(BF16)
        g_ple = norm_ple[i][None, :]
        hp = _post(a_p, proj_p, z_col, hp, pp, i, w_o, w_p, w_g, g_ple, g_final)
        hs = _post(a_s, proj_s, z_col, hs, ps, i, w_o, w_p, w_g, g_ple, g_final)

    st = lambda key: jnp.stack(outs[key])

    def st_t(key):
        x = jnp.stack(outs[key]).reshape(-1, batch, N_HEADS, HEAD_DIM, seq)
        return jnp.transpose(x, (0, 1, 4, 2, 3))

    return (hp.reshape(batch, seq, d), hs.reshape(dec_b, dec_t, d),
            st_t("kf_p"), st_t("vf_p"), st("lf_p"), st_t("km_p"), st_t("vm_p"), st("pl_p"),
            st("kf_s"), st("vf_s"), st("lf_s"), st("km_s"), st("vm_s"), st("pl_s"))
```

```python
import functools

import jax
import jax.numpy as jnp
from jax import lax
from jax.experimental import pallas as pl
from jax.experimental.pallas import tpu as pltpu

F32 = jnp.float32
BF16 = jnp.bfloat16

D_MODEL = 1024
N_HEADS = 16
HEAD_DIM = 64
BRANCH = N_HEADS * HEAD_DIM
PAGE_SIZE = 128
MOBA_BLOCK = 256
MOBA_TOPK = 3
POOL_WINDOWS = (2, 4, 8, 16)
POOL_GROUP_W = BRANCH // len(POOL_WINDOWS)
POOL_HIST = max(POOL_WINDOWS) - 1
N_MIXERS = 3
RMS_EPS = 1e-6
NEG_INF = -1e30
ATTN_SCALE = HEAD_DIM ** -0.5

V7X_LANES = 128
V7X_SUBLANES = 8
V7X_VMEM_BYTES = 64 * 1024 * 1024
VMEM_CEILING = V7X_VMEM_BYTES - 8 * 1024 * 1024

HEADS_PER_VREG = V7X_LANES // HEAD_DIM
N_HEAD_PAIRS = N_HEADS // HEADS_PER_VREG
PAGES_PER_STEP = 16
ROW_TILE = 512
PROJ_ROW_TILE = 1024
ATTN_TILE = MOBA_BLOCK


def _vmem_limit(block_bytes, scratch_bytes=0):
    est = 2 * block_bytes + scratch_bytes
    return int(min(VMEM_CEILING, max(32 * 1024 * 1024, est * 3 // 2)))


def _nt_dot(a, b):
    return lax.dot_general(a, b, (((1,), (1,)), ((), ())), preferred_element_type=F32)


def _rmsnorm(x, g):
    return x * lax.rsqrt(jnp.mean(x * x, axis=-1, keepdims=True) + RMS_EPS) * g


def _split_bf16(x):
    hi = x.astype(BF16)
    lo = (x - hi.astype(F32)).astype(BF16)
    return hi, lo


def _norm_proj_body(*refs, has_forget):
    if has_forget:
        x_ref, g_ref, w_ref, wf_ref, bf_ref, o_ref, lf_ref, xn_ref = refs
    else:
        x_ref, g_ref, w_ref, o_ref, xn_ref = refs

    @pl.when(pl.program_id(1) == 0)
    def _():
        xn = _rmsnorm(x_ref[...], g_ref[...]).astype(BF16)
        xn_ref[...] = xn
        if has_forget:
            raw = jnp.dot(xn, wf_ref[...], preferred_element_type=F32) + bf_ref[...]
            lf_ref[...] = jax.nn.log_sigmoid(raw)

    o_ref[...] = jnp.dot(xn_ref[...], w_ref[...], preferred_element_type=F32)


def _norm_proj(x, g, w, wf=None, bf=None):
    n, d = x.shape
    n_out = w.shape[1]
    tm = min(n, PROJ_ROW_TILE)
    tn = min(n_out, 2048)
    assert n % tm == 0 and n_out % tn == 0
    has_forget = wf is not None
    in_specs = [
        pl.BlockSpec((tm, d), lambda i, j: (i, 0)),
        pl.BlockSpec((1, d), lambda i, j: (0, 0)),
        pl.BlockSpec((d, tn), lambda i, j: (0, j)),
    ]
    out_shape = [jax.ShapeDtypeStruct((n, n_out), F32)]
    out_specs = [pl.BlockSpec((tm, tn), lambda i, j: (i, j))]
    args = [x, g, w]
    block_bytes = tm * d * 4 + d * tn * 2 + tm * tn * 4
    if has_forget:
        in_specs += [pl.BlockSpec((d, V7X_LANES), lambda i, j: (0, 0)),
                     pl.BlockSpec((1, V7X_LANES), lambda i, j: (0, 0))]
        out_shape.append(jax.ShapeDtypeStruct((n, V7X_LANES), F32))
        out_specs.append(pl.BlockSpec((tm, V7X_LANES), lambda i, j: (i, 0)))
        args += [wf, bf]
        block_bytes += d * V7X_LANES * 2 + tm * V7X_LANES * 4
    res = pl.pallas_call(
        functools.partial(_norm_proj_body, has_forget=has_forget),
        grid=(n // tm, n_out // tn),
        in_specs=in_specs,
        out_specs=out_specs,
        out_shape=out_shape,
        scratch_shapes=[pltpu.VMEM((tm, d), BF16)],
        compiler_params=pltpu.CompilerParams(
            dimension_semantics=("arbitrary", "arbitrary"),
            vmem_limit_bytes=_vmem_limit(block_bytes, tm * d * 2)),
        name="norm_proj_forget" if has_forget else "norm_proj",
    )(*args)
    return res if has_forget else res[0]


def _post_body(*refs, final_norm):
    if final_norm:
        (a_ref, z_ref, h_ref, p_ref, wo_ref, wp_ref, wg_ref, g_ref, gf_ref, o_ref) = refs
    else:
        (a_ref, z_ref, h_ref, p_ref, wo_ref, wp_ref, wg_ref, g_ref, o_ref) = refs
    gated = (a_ref[...] * jax.nn.silu(z_ref[...])).astype(BF16)
    h1 = h_ref[...] + jnp.dot(gated, wo_ref[...], preferred_element_type=F32)
    hn = _rmsnorm(h1, g_ref[...]).astype(BF16)
    gate = jax.nn.sigmoid(jnp.dot(hn, wg_ref[...], preferred_element_type=F32))
    pp = jnp.dot(p_ref[...].astype(BF16), wp_ref[...], preferred_element_type=F32)
    h2 = h1 + pp * gate
    if final_norm:
        h2 = _rmsnorm(h2, gf_ref[...])
    o_ref[...] = h2


def _post(a, proj, z_col, h, p, layer, w_out, w_ple, w_gate, g_ple, g_final=None):
    n, d = h.shape
    tm = ROW_TILE
    ple = p.shape[2]
    final_norm = g_final is not None
    row = lambda i: (i, 0)
    const = lambda i: (0, 0)
    in_specs = [
        pl.BlockSpec((tm, BRANCH), row),
        pl.BlockSpec((tm, BRANCH), lambda i: (i, z_col)),
        pl.BlockSpec((tm, d), row),
        pl.BlockSpec((None, tm, ple), lambda i: (layer, i, 0)),
        pl.BlockSpec((BRANCH, d), const),
        pl.BlockSpec((ple, d), const),
        pl.BlockSpec((d, d), const),
        pl.BlockSpec((1, d), const),
    ]
    args = [a, proj, h, p, w_out, w_ple, w_gate, g_ple]
    if final_norm:
        in_specs.append(pl.BlockSpec((1, d), const))
        args.append(g_final)
    block_bytes = (3 * tm * d + tm * ple + tm * d) * 4 + (BRANCH * d + ple * d + d * d) * 2
    return pl.pallas_call(
        functools.partial(_post_body, final_norm=final_norm),
        grid=(n // tm,),
        in_specs=in_specs,
        out_specs=pl.BlockSpec((tm, d), row),
        out_shape=jax.ShapeDtypeStruct((n, d), F32),
        compiler_params=pltpu.CompilerParams(
            dimension_semantics=("arbitrary",),
            vmem_limit_bytes=_vmem_limit(block_bytes)),
        name="post_final" if final_norm else "post",
    )(*args)


def _fox_decay_body(lf_ref, c_ref):
    seq = lf_ref.shape[0]
    x = lf_ref[...].T[:N_HEADS, :]
    lane = lax.broadcasted_iota(jnp.int32, x.shape, 1)
    shift = 1
    while shift < seq:
        x = x + jnp.where(lane >= shift, pltpu.roll(x, shift, 1), 0.0)
        shift *= 2
    x = jnp.concatenate([x, jnp.zeros((V7X_LANES - N_HEADS, seq), F32)], axis=0)
    c_ref[...] = x.T


def _fox_decay(lf):
    b, s, _ = lf.shape
    return pl.pallas_call(
        _fox_decay_body,
        grid=(b,),
        in_specs=[pl.BlockSpec((None, s, V7X_LANES), lambda i: (i, 0, 0))],
        out_specs=pl.BlockSpec((None, s, V7X_LANES), lambda i: (i, 0, 0)),
        out_shape=jax.ShapeDtypeStruct((b, s, V7X_LANES), F32),
        compiler_params=pltpu.CompilerParams(
            dimension_semantics=("arbitrary",),
            vmem_limit_bytes=_vmem_limit(s * V7X_LANES * 4 * 2, s * V7X_LANES * 4 * 4)),
        name="fox_decay",
    )(lf)


def _prompt_attn_body(*refs, kind):
    if kind == "fox":
        (q_ref, k_ref, v_ref, c_ref, o_ref, kt_out_ref, vt_out_ref,
         kb_ref, vt_ref, qm_ref, acc_ref, m_ref, l_ref, bias_ref) = refs
    else:
        (slope_ref, q_ref, k_ref, v_ref, o_ref, kt_out_ref, vt_out_ref,
         kb_ref, vt_ref, qm_ref, acc_ref, m_ref, l_ref, sel_ref, kmean_ref, bias_ref) = refs
    hp = pl.program_id(1)
    n_tiles, tile, _ = kb_ref.shape
    nh = HEADS_PER_VREG

    def slot(j, i):
        return pl.ds((nh * j + i) * tile, tile)

    lane = lax.broadcasted_iota(jnp.int32, (tile, V7X_LANES), 1)
    in_head = (lane < HEAD_DIM, lane >= HEAD_DIM)
    if kind == "moba":
        kmean_ref[...] = jnp.zeros(kmean_ref.shape, F32)
    for j in range(n_tiles):
        rows = pl.ds(j * tile, tile)
        kf = k_ref[rows, :]
        kb_ref[j] = kf.astype(BF16)
        kt_out_ref[:, rows] = kf.T
        vt = v_ref[rows, :].T
        vt_out_ref[:, rows] = vt
        vt_ref[j] = vt.astype(BF16)
        q = q_ref[rows, :]
        for i in range(HEADS_PER_VREG):
            qm_ref[slot(j, i), :] = (jnp.where(in_head[i], q, 0.0) * ATTN_SCALE).astype(BF16)
        if kind == "moba":
            kmean_ref[pl.ds(j, 1), :] = jnp.mean(kf, axis=0, keepdims=True)

    if kind == "fox":
        src = lax.broadcasted_iota(jnp.int32, (V7X_LANES, V7X_LANES), 0)
        for j in range(n_tiles):
            c = c_ref[pl.ds(j * tile, tile), :]
            c_hi, c_rest = _split_bf16(c)
            c_mid, c_lo = _split_bf16(c - c_hi.astype(F32))
            del c_rest
            for i in range(HEADS_PER_VREG):
                pick = jnp.where(src == HEADS_PER_VREG * hp + i, 1.0, 0.0).astype(BF16)
                dot = lambda x: jnp.dot(x, pick, preferred_element_type=F32)
                bias_ref[i, j] = dot(c_hi) + (dot(c_mid) + dot(c_lo))
    else:
        km_hi, km_lo = _split_bf16(kmean_ref[...])
        jrow = lax.broadcasted_iota(jnp.int32, (kmean_ref.shape[0], tile), 0)
        for j in range(n_tiles):
            q = q_ref[pl.ds(j * tile, tile), :]
            valid = jrow < j
            for i in range(HEADS_PER_VREG):
                q_hi, q_lo = _split_bf16(jnp.where(in_head[i], q, 0.0))
                gate = _nt_dot(km_hi, q_hi) + (_nt_dot(km_hi, q_lo) + _nt_dot(km_lo, q_hi))
                gate = jnp.where(valid, gate, NEG_INF)
                rank = jnp.zeros(gate.shape, jnp.int32)
                for jp in range(j):
                    gj = gate[jp:jp + 1, :]
                    beats = (gj > gate) | ((gj == gate) & (jp < jrow))
                    rank = rank + beats.astype(jnp.int32)
                sel_ref[:, slot(j, i)] = jnp.where(valid & (rank < MOBA_TOPK), 1.0, 0.0)
        rel = (lax.broadcasted_iota(jnp.int32, (tile, tile), 1)
               - lax.broadcasted_iota(jnp.int32, (tile, tile), 0)).astype(F32)
        for j in range(n_tiles):
            for i in range(nh):
                bias_ref[:, slot(j, i)] = slope_ref[nh * hp + i] * (rel + float(j * tile))

    own_w = nh * tile
    kloc = lax.broadcasted_iota(jnp.int32, (tile, own_w), 0)
    qloc = lax.broadcasted_iota(jnp.int32, (tile, own_w), 1) % tile
    for kj in range(n_tiles):
        width = nh * (n_tiles - kj) * tile
        queries = pl.ds(nh * kj * tile, width)
        st = _nt_dot(kb_ref[kj], qm_ref[queries, :])
        if kind == "fox":
            per_tile = [bias_ref[i, kj] for i in range(nh) for _ in range(tile // V7X_LANES)]
            st = st - jnp.concatenate(per_tile * (n_tiles - kj), axis=1)
        else:
            st = st - bias_ref[:, pl.ds(0, width)]
        own = jnp.where(kloc <= qloc, st[:, :own_w], NEG_INF)
        if width == own_w:
            st = own
        elif kind == "fox":
            st = jnp.concatenate([own, st[:, own_w:]], axis=1)
        else:
            picked = sel_ref[pl.ds(kj, 1), pl.ds(nh * (kj + 1) * tile, width - own_w)] > 0.5
            st = jnp.concatenate([own, jnp.where(picked, st[:, own_w:], NEG_INF)], axis=1)
        m_new = jnp.max(st, axis=0, keepdims=True)
        if kj > 0:
            m_old = m_ref[:, queries]
            m_new = jnp.maximum(m_old, m_new)
        p = jnp.exp(st - m_new)
        if kind == "moba" and width > own_w:
            p = jnp.concatenate([p[:, :own_w], jnp.where(picked, p[:, own_w:], 0.0)], axis=1)
        l_new = jnp.sum(p, axis=0, keepdims=True)
        pv = jnp.dot(vt_ref[kj], p.astype(BF16), preferred_element_type=F32)
        if kj > 0:
            alpha = jnp.exp(m_old - m_new)
            l_new = alpha * l_ref[:, queries] + l_new
            pv = alpha * acc_ref[:, queries] + pv
        m_ref[:, queries] = m_new
        l_ref[:, queries] = l_new
        acc_ref[:, queries] = pv

    row = lax.broadcasted_iota(jnp.int32, (V7X_LANES, tile), 0)
    for j in range(n_tiles):
        out_t = jnp.where(row < HEAD_DIM, acc_ref[:, slot(j, 0)] / l_ref[:, slot(j, 0)],
                          acc_ref[:, slot(j, 1)] / l_ref[:, slot(j, 1)])
        o_ref[pl.ds(j * tile, tile), :] = out_t.T


def _prompt_attention(kind, proj, aux):
    b, s, _ = proj.shape
    tile = ATTN_TILE
    n_tiles = s // tile
    cols = BRANCH // V7X_LANES
    seq_block = lambda col0: pl.BlockSpec((None, s, V7X_LANES), lambda bi, hp: (bi, 0, col0 + hp))
    in_specs = [seq_block(0), seq_block(cols), seq_block(2 * cols)]
    args = [proj, proj, proj]
    state = lambda rows: pltpu.VMEM((rows, HEADS_PER_VREG * s), F32)
    scratch = [
        pltpu.VMEM((n_tiles, tile, V7X_LANES), BF16),
        pltpu.VMEM((n_tiles, V7X_LANES, tile), BF16),
        pltpu.VMEM((HEADS_PER_VREG * s, V7X_LANES), BF16),
        state(V7X_LANES), state(1), state(1),
    ]
    if kind == "fox":
        in_specs.append(pl.BlockSpec((None, s, V7X_LANES), lambda bi, hp: (bi, 0, 0)))
        args.append(aux)
        scratch.append(pltpu.VMEM((HEADS_PER_VREG, n_tiles, tile, V7X_LANES), F32))
    else:
        in_specs.insert(0, pl.BlockSpec(memory_space=pltpu.SMEM))
        args.insert(0, aux)
        scratch += [state(2 * V7X_SUBLANES), pltpu.VMEM((2 * V7X_SUBLANES, V7X_LANES), F32), state(tile)]
    block_bytes = 7 * s * V7X_LANES * 4
    scratch_bytes = 32 * s * V7X_LANES * 4
    transposed = pl.BlockSpec((None, V7X_LANES, s), lambda bi, hp: (bi, hp, 0))
    return pl.pallas_call(
        functools.partial(_prompt_attn_body, kind=kind),
        grid=(b, N_HEAD_PAIRS),
        in_specs=in_specs,
        out_specs=[pl.BlockSpec((None, s, V7X_LANES), lambda bi, hp: (bi, 0, hp)), transposed, transposed],
        out_shape=[jax.ShapeDtypeStruct((b, s, BRANCH), F32),
                   jax.ShapeDtypeStruct((b, BRANCH, s), F32),
                   jax.ShapeDtypeStruct((b, BRANCH, s), F32)],
        scratch_shapes=scratch,
        compiler_params=pltpu.CompilerParams(
            dimension_semantics=("arbitrary", "arbitrary"),
            vmem_limit_bytes=_vmem_limit(block_bytes, scratch_bytes)),
        name=kind + "_prompt",
    )(*args)


def _pool_body(*refs, start_pos, has_hist):
    if has_hist:
        u_ref, hist_ref, wg_ref, sc_ref, y_ref, ho_ref, ext_ref = refs
    else:
        u_ref, wg_ref, sc_ref, y_ref, ho_ref, ext_ref = refs
    t = pl.program_id(1)
    ts = u_ref.shape[0]
    pad = POOL_HIST + 1

    @pl.when(t == 0)
    def _():
        if has_hist:
            ext_ref[0:1, :] = jnp.zeros((1, BRANCH), F32)
            ext_ref[1:pad, :] = hist_ref[...]
        else:
            ext_ref[0:pad, :] = jnp.zeros((pad, BRANCH), F32)

    @pl.when(t > 0)
    def _():
        ext_ref[0:pad, :] = ext_ref[ts:ts + pad, :]

    u = u_ref[...]
    ext_ref[pad:pad + ts, :] = u
    pos = start_pos + t * ts + lax.broadcasted_iota(jnp.int32, (ts, 1), 0)
    for gi, w in enumerate(POOL_WINDOWS):
        c0 = gi * POOL_GROUP_W
        win = ext_ref[pl.ds(pad, ts), pl.ds(c0, POOL_GROUP_W)]
        for back in range(1, w):
            win = win + ext_ref[pl.ds(pad - back, ts), pl.ds(c0, POOL_GROUP_W)]
        cnt = jnp.minimum(pos + 1, w).astype(F32)
        d = win / cnt - u[:, c0:c0 + POOL_GROUP_W]
        y = jnp.dot(d.astype(BF16), wg_ref[gi], preferred_element_type=F32)
        y_ref[:, c0:c0 + POOL_GROUP_W] = y * sc_ref[:, c0:c0 + POOL_GROUP_W]

    @pl.when(t == pl.num_programs(1) - 1)
    def _():
        ho_ref[...] = ext_ref[ts + 1:ts + pad, :]


def _pool(proj, hist, w_group, scale, start_pos):
    b, t, _ = proj.shape
    ts = min(t, ROW_TILE)
    has_hist = hist is not None
    pad = POOL_HIST + 1
    in_specs = [pl.BlockSpec((None, ts, BRANCH), lambda bi, ti: (bi, ti, 0))]
    args = [proj]
    if has_hist:
        in_specs.append(pl.BlockSpec((None, POOL_HIST, BRANCH), lambda bi, ti: (bi, 0, 0)))
        args.append(hist)
    in_specs += [
        pl.BlockSpec((len(POOL_WINDOWS), POOL_GROUP_W, POOL_GROUP_W), lambda bi, ti: (0, 0, 0)),
        pl.BlockSpec((1, BRANCH), lambda bi, ti: (0, 0)),
    ]
    args += [w_group, scale]
    block_bytes = (2 * ts * BRANCH + 2 * pad * BRANCH) * 4 + w_group.size * 2
    return pl.pallas_call(
        functools.partial(_pool_body, start_pos=start_pos, has_hist=has_hist),
        grid=(b, t // ts),
        in_specs=in_specs,
        out_specs=[
            pl.BlockSpec((None, ts, BRANCH), lambda bi, ti: (bi, ti, 0)),
            pl.BlockSpec((None, POOL_HIST, BRANCH), lambda bi, ti: (bi, 0, 0)),
        ],
        out_shape=[
            jax.ShapeDtypeStruct((b, t, BRANCH), F32),
            jax.ShapeDtypeStruct((b, POOL_HIST, BRANCH), F32),
        ],
        scratch_shapes=[pltpu.VMEM((ts + pad, BRANCH), F32)],
        compiler_params=pltpu.CompilerParams(
            dimension_semantics=("arbitrary", "arbitrary"),
            vmem_limit_bytes=_vmem_limit(block_bytes, (ts + pad) * BRANCH * 4)),
        name="pool_hist" if has_hist else "pool",
    )(*args)


def _lane_cumsum(x):
    lane = lax.broadcasted_iota(jnp.int32, x.shape, 1)
    shift = 1
    while shift < x.shape[1]:
        x = x + jnp.where(lane >= shift, pltpu.roll(x, shift, 1), 0.0)
        shift *= 2
    return x


def _lane_total(x):
    shift = 1
    while shift < x.shape[1]:
        x = x + pltpu.roll(x, shift, 1)
        shift *= 2
    return x


def _page_matrix(block):
    return block.reshape(BRANCH, PAGE_SIZE).astype(BF16)


def _new_token_page(x):
    x = jnp.concatenate([x, jnp.zeros((PAGE_SIZE - x.shape[0], x.shape[1]), F32)], axis=0)
    return x.T.astype(BF16)


def _sample_attn_body(pt_ref, q_ref, kn_ref, vn_ref, aux_ref, *refs, kind, n_q, n_pages, past_len):
    del pt_ref
    n = PAGES_PER_STEP
    k_refs, v_refs, rest = refs[:n], refs[n:2 * n], refs[2 * n:]
    if kind == "fox":
        lf_refs, rest = rest[:n], rest[n:]
    o_ref, qbd_ref, s_ref, side_ref, l_ref, acc_ref = rest
    step = pl.program_id(1)
    k_steps = n_pages // n
    rows = n_q * N_HEADS
    row = lax.broadcasted_iota(jnp.int32, (rows, PAGE_SIZE), 0)
    lane = lax.broadcasted_iota(jnp.int32, (rows, PAGE_SIZE), 1)
    t_row = row // N_HEADS

    @pl.when(step == 0)
    def _():
        q = q_ref[...]
        if kind == "fox":
            q = q * ATTN_SCALE
        d_of_col = lax.broadcasted_iota(jnp.int32, (HEAD_DIM, BRANCH), 1) % HEAD_DIM
        spread = jnp.where(lax.broadcasted_iota(jnp.int32, (HEAD_DIM, BRANCH), 0) == d_of_col, 1.0, 0.0)
        q_all = jnp.dot(q.astype(BF16), spread.astype(BF16), preferred_element_type=F32)
        r = lax.broadcasted_iota(jnp.int32, (rows, BRANCH), 0)
        c = lax.broadcasted_iota(jnp.int32, (rows, BRANCH), 1)
        qbd_ref[...] = jnp.where(c // HEAD_DIM == r % N_HEADS, q_all, 0.0).astype(BF16)
        acc_ref[...] = jnp.zeros(acc_ref.shape, F32)
        if kind == "fox":
            side_ref[...] = jnp.zeros(side_ref.shape, F32)

    @pl.when(step < k_steps)
    def _():
        for r in range(n):
            pg = step * n + r
            raw = jnp.dot(qbd_ref[...], _page_matrix(k_refs[r][...]), preferred_element_type=F32)
            if kind == "fox":
                lf = lf_refs[r][...]
                c = _lane_cumsum(lf) + side_ref[...]
                side_ref[...] = side_ref[...] + _lane_total(lf)
                s_ref[pg] = raw - jnp.concatenate([c] * n_q, axis=0)
            else:
                s_ref[pg] = raw
                side_ref[pg] = jnp.sum(raw, axis=1, keepdims=True)

    @pl.when(step == k_steps - 1)
    def _():
        raw_n = jnp.dot(qbd_ref[...], _new_token_page(kn_ref[...]), preferred_element_type=F32)
        causal = (lane < n_q) & (lane <= t_row)
        if kind == "fox":
            c_n = _lane_cumsum(aux_ref[...]) + side_ref[...]
            s_n = raw_n - jnp.concatenate([c_n] * n_q, axis=0)
        else:
            slope = aux_ref[...]
            per_block = MOBA_BLOCK // PAGE_SIZE
            n_blocks = n_pages // per_block
            gates = []
            for j in range(n_blocks):
                tot = functools.reduce(jnp.add, [side_ref[j * per_block + e] for e in range(per_block)])
                gates.append(tot * (1.0 / MOBA_BLOCK))
            for j in range(n_blocks):
                rank = jnp.zeros(gates[j].shape, jnp.int32)
                for jp in range(n_blocks):
                    if jp != j:
                        beats = (gates[jp] > gates[j]) | ((gates[jp] == gates[j]) & (jp < j))
                        rank = rank + beats.astype(jnp.int32)
                sel = rank < MOBA_TOPK
                for pg in range(j * per_block, (j + 1) * per_block):
                    dist = (past_len - pg * PAGE_SIZE + t_row - lane).astype(F32)
                    s_ref[pg] = jnp.where(sel, s_ref[pg] * ATTN_SCALE - slope * dist, NEG_INF)
            s_n = raw_n * ATTN_SCALE - slope * (t_row - lane).astype(F32)
        s_ref[n_pages] = jnp.where(causal, s_n, NEG_INF)
        m = functools.reduce(jnp.maximum, [s_ref[pg] for pg in range(n_pages + 1)])
        m = jnp.max(m, axis=1, keepdims=True)
        tot = jnp.zeros((rows, PAGE_SIZE), F32)
        for pg in range(n_pages + 1):
            e = jnp.exp(s_ref[pg] - m)
            s_ref[pg] = e
            tot = tot + e
        l_ref[...] = jnp.sum(tot, axis=1, keepdims=True)

    @pl.when(step >= k_steps)
    def _():
        for r in range(n):
            p = s_ref[(step - k_steps) * n + r].astype(BF16)
            acc_ref[...] += _nt_dot(p, _page_matrix(v_refs[r][...]))

    @pl.when(step == 2 * k_steps - 1)
    def _():
        acc = acc_ref[...] + _nt_dot(s_ref[n_pages].astype(BF16), _new_token_page(vn_ref[...]))
        head = lax.broadcasted_iota(jnp.int32, (rows, V7X_LANES), 0) % N_HEADS
        picked = jnp.zeros((rows, V7X_LANES), F32)
        for j in range(N_HEAD_PAIRS):
            picked = jnp.where(head // HEADS_PER_VREG == j, acc[:, j * V7X_LANES:(j + 1) * V7X_LANES], picked)
        picked = jnp.where(head % HEADS_PER_VREG == 0, picked, pltpu.roll(picked, HEAD_DIM, 1))
        o_ref[...] = picked[:, :HEAD_DIM] / l_ref[...]


def _page_spec(block, slot, k_phase, r, k_steps):
    n = PAGES_PER_STEP
    zeros = (0,) * (len(block) - 2)

    def index_map(b, s, pt):
        if k_phase:
            row, grp = b, jnp.minimum(s, k_steps - 1)
        else:
            in_phase = s >= k_steps
            row = jnp.where(in_phase, b, jnp.maximum(b - 1, 0))
            grp = jnp.where(in_phase, s - k_steps, k_steps - 1)
        return (slot, pt[row, grp * n + r]) + zeros

    return pl.BlockSpec(block, index_map)


def _sample_attention(kind, page_table, q, k_new, v_new, aux, cache_k, cache_v, slot, *, cache_lf=None,
                      past_len=None):
    db, rows, hd = q.shape
    n_q = rows // N_HEADS
    n_pages = page_table.shape[1]
    n = PAGES_PER_STEP
    k_steps = n_pages // n
    page_block = (None, None, N_HEADS, HEAD_DIM, PAGE_SIZE)
    per_b = lambda b, s, pt: (b, 0, 0)
    in_specs = [pl.BlockSpec((None, rows, hd), per_b),
                pl.BlockSpec((None,) + k_new.shape[1:], per_b),
                pl.BlockSpec((None,) + v_new.shape[1:], per_b)]
    if kind == "fox":
        in_specs.append(pl.BlockSpec((None, N_HEADS, PAGE_SIZE), per_b))
        side = pltpu.VMEM((N_HEADS, PAGE_SIZE), F32)
    else:
        in_specs.append(pl.BlockSpec((rows, PAGE_SIZE), lambda b, s, pt: (0, 0)))
        side = pltpu.VMEM((n_pages, rows, 1), F32)
    in_specs += [_page_spec(page_block, slot, True, r, k_steps) for r in range(n)]
    in_specs += [_page_spec(page_block, slot, False, r, k_steps) for r in range(n)]
    args = [q, k_new, v_new, aux] + [cache_k] * n + [cache_v] * n
    if kind == "fox":
        in_specs += [_page_spec((None, None, N_HEADS, PAGE_SIZE), slot, True, r, k_steps) for r in range(n)]
        args += [cache_lf] * n
    page_bytes = BRANCH * PAGE_SIZE * 4
    block_bytes = 2 * n * page_bytes + 4 * rows * BRANCH * 4
    scratch_bytes = (n_pages + 1) * rows * PAGE_SIZE * 4 + rows * BRANCH * 8 + 6 * page_bytes
    return pl.pallas_call(
        functools.partial(_sample_attn_body, kind=kind, n_q=n_q, n_pages=n_pages, past_len=past_len),
        grid_spec=pltpu.PrefetchScalarGridSpec(
            num_scalar_prefetch=1,
            grid=(db, 2 * k_steps),
            in_specs=in_specs,
            out_specs=pl.BlockSpec((None, rows, hd), per_b),
            scratch_shapes=[
                pltpu.VMEM((rows, BRANCH), BF16),
                pltpu.VMEM((n_pages + 1, rows, PAGE_SIZE), F32),
                side,
                pltpu.VMEM((rows, 1), F32),
                pltpu.VMEM((rows, BRANCH), F32),
            ],
        ),
        out_shape=jax.ShapeDtypeStruct((db, rows, hd), F32),
        compiler_params=pltpu.CompilerParams(
            dimension_semantics=("arbitrary", "arbitrary"),
            vmem_limit_bytes=_vmem_limit(block_bytes, scratch_bytes)),
        name=kind + "_sample",
    )(page_table, *args)


def kernel(x_prompt, x_sample, cache_k_fox, cache_v_fox, cache_logf_fox, cache_k_moba, cache_v_moba, state_pool, page_table, p_prompt, p_sample, norm_mix, w_in_fox, b_forget, w_in_moba, w_in_pool, w_pool_group, pool_scale, w_out, w_ple, norm_ple, w_ple_gate, norm_final):
    batch, seq, d = x_prompt.shape
    dec_b, dec_t, _ = x_sample.shape
    depth = norm_mix.shape[0]
    n_pages = page_table.shape[1]
    past_len = n_pages * PAGE_SIZE
    n_p, n_s = batch * seq, dec_b * dec_t
    assert d == D_MODEL and n_p % ROW_TILE == 0 and n_s % ROW_TILE == 0 and seq % MOBA_BLOCK == 0
    assert past_len % MOBA_BLOCK == 0 and n_pages % PAGES_PER_STEP == 0 and dec_t <= V7X_SUBLANES

    hp = x_prompt.reshape(n_p, d)
    hs = x_sample.reshape(n_s, d)
    pp = p_prompt.reshape(depth, n_p, -1)
    ps = p_sample.reshape(depth, n_s, -1)
    slopes = 2.0 ** (-8.0 * jnp.arange(1, N_HEADS + 1, dtype=F32) / N_HEADS)
    slope_rows = jnp.broadcast_to(jnp.tile(slopes, dec_t)[:, None], (dec_t * N_HEADS, PAGE_SIZE))
    tok_minor = lambda c: jnp.transpose(c, (0, 1, 3, 4, 2))
    kf_cache, vf_cache = tok_minor(cache_k_fox), tok_minor(cache_v_fox)
    km_cache, vm_cache = tok_minor(cache_k_moba), tok_minor(cache_v_moba)
    lf_cache = jnp.transpose(cache_logf_fox, (0, 1, 3, 2))

    def heads(t2d, lead):
        return t2d.reshape(lead + (N_HEADS, HEAD_DIM))

    def sample_q(t2d):
        return t2d.reshape(dec_b, dec_t * N_HEADS, HEAD_DIM)

    def sample_new(t2d):
        return jnp.pad(t2d.reshape(dec_b, dec_t, BRANCH), ((0, 0), (0, V7X_SUBLANES - dec_t), (0, 0)))

    outs = {k: [] for k in ("kf_p", "vf_p", "lf_p", "km_p", "vm_p", "pl_p",
                            "kf_s", "vf_s", "lf_s", "km_s", "vm_s", "pl_s")}
    for i in range(depth):
        kind, slot = i % N_MIXERS, i // N_MIXERS
        g_mix = norm_mix[i][None, :]
        if kind == 0:
            w_main = w_in_fox[slot][:, :4 * BRANCH].astype(BF16)
            wf = jnp.pad(w_in_fox[slot][:, 4 * BRANCH:], ((0, 0), (0, V7X_LANES - N_HEADS))).astype(BF16)
            bf = jnp.pad(b_forget[slot], (0, V7X_LANES - N_HEADS))[None, :]
            proj_p, lf_p = _norm_proj(hp, g_mix, w_main, wf, bf)
            proj_s, lf_s = _norm_proj(hs, g_mix, w_main, wf, bf)
            c_p = _fox_decay(lf_p.reshape(batch, seq, V7X_LANES))
            a_p, kt_p, vt_p = _prompt_attention("fox", proj_p.reshape(batch, seq, 4 * BRANCH), c_p)
            a_p = a_p.reshape(n_p, BRANCH)
            lf_new = jnp.pad(jnp.transpose(lf_s[:, :N_HEADS].reshape(dec_b, dec_t, N_HEADS), (0, 2, 1)),
                             ((0, 0), (0, 0), (0, PAGE_SIZE - dec_t)))
            a_s = _sample_attention(
                "fox", page_table, sample_q(proj_s[:, :BRANCH]), sample_new(proj_s[:, BRANCH:2 * BRANCH]),
                sample_new(proj_s[:, 2 * BRANCH:3 * BRANCH]), lf_new, kf_cache, vf_cache, slot,
                cache_lf=lf_cache).reshape(n_s, BRANCH)
            outs["kf_p"].append(kt_p)
            outs["vf_p"].append(vt_p)
            outs["lf_p"].append(lf_p[:, :N_HEADS].reshape(batch, seq, N_HEADS))
            outs["kf_s"].append(heads(proj_s[:, BRANCH:2 * BRANCH], (dec_b, dec_t)))
            outs["vf_s"].append(heads(proj_s[:, 2 * BRANCH:3 * BRANCH], (dec_b, dec_t)))
            outs["lf_s"].append(lf_s[:, :N_HEADS].reshape(dec_b, dec_t, N_HEADS))
            z_col = 3
        elif kind == 1:
            w_main = w_in_moba[slot].astype(BF16)
            proj_p = _norm_proj(hp, g_mix, w_main)
            proj_s = _norm_proj(hs, g_mix, w_main)
            a_p, kt_p, vt_p = _prompt_attention("moba", proj_p.reshape(batch, seq, 4 * BRANCH), slopes)
            a_p = a_p.reshape(n_p, BRANCH)
            a_s = _sample_attention(
                "moba", page_table, sample_q(proj_s[:, :BRANCH]), sample_new(proj_s[:, BRANCH:2 * BRANCH]),
                sample_new(proj_s[:, 2 * BRANCH:3 * BRANCH]), slope_rows, km_cache, vm_cache, slot,
                past_len=past_len).reshape(n_s, BRANCH)
            outs["km_p"].append(kt_p)
            outs["vm_p"].append(vt_p)
            outs["km_s"].append(heads(proj_s[:, BRANCH:2 * BRANCH], (dec_b, dec_t)))
            outs["vm_s"].append(heads(proj_s[:, 2 * BRANCH:3 * BRANCH], (dec_b, dec_t)))
            z_col = 3
        else:
            w_main = w_in_pool[slot].astype(BF16)
            wg = w_pool_group[slot].astype(BF16)
            sc = pool_scale[slot][None, :]
            proj_p = _norm_proj(hp, g_mix, w_main)
            proj_s = _norm_proj(hs, g_mix, w_main)
            a_p, hist_p = _pool(proj_p.reshape(batch, seq, 2 * BRANCH), None, wg, sc, 0)
            a_s, hist_s = _pool(proj_s.reshape(dec_b, dec_t, 2 * BRANCH), state_pool[slot], wg, sc, past_len)
            a_p, a_s = a_p.reshape(n_p, BRANCH), a_s.reshape(n_s, BRANCH)
            outs["pl_p"].append(hist_p)
            outs["pl_s"].append(hist_s)
            z_col = 1
        g_final = norm_final[None, :] if i == depth - 1 else None
        w_o, w_p, w_g = w_out[i].astype(BF16), w_ple[i].astype(BF16), w_ple_gate[i].astype(BF16)
        g_ple = norm_ple[i][None, :]
        hp = _post(a_p, proj_p, z_col, hp, pp, i, w_o, w_p, w_g, g_ple, g_final)
        hs = _post(a_s, proj_s, z_col, hs, ps, i, w_o, w_p, w_g, g_ple, g_final)

    st = lambda key: jnp.stack(outs[key])

    def st_t(key):
        x = jnp.stack(outs[key]).reshape(-1, batch, N_HEADS, HEAD_DIM, seq)
        return jnp.transpose(x, (0, 1, 4, 2, 3))

    return (hp.reshape(batch, seq, d), hs.reshape(dec_b, dec_t, d),
            st_t("kf_p"), st_t("vf_p"), st("lf_p"), st_t("km_p"), st_t("vm_p"), st("pl_p"),
            st("kf_s"), st("vf_s"), st("lf_s"), st("km_s"), st("vm_s"), st("pl_s"))
```
